```python
import math
import jax, jax.numpy as jnp
from jax import lax
import numpy as np

D_MODEL = 1024
BATCH = 16
SEQ = 2048
DEPTH = 2
DEC_BATCH = 128
DEC_SEQ = 8
PAST_LEN = 16384
PAGE_SIZE = 128

N_A_LAYERS = DEPTH // 2
N_B_LAYERS = DEPTH - N_A_LAYERS
A_HEADS = 8
A_DK = 128
A_DV = 128
A_KD = A_HEADS * A_DK
A_VD = A_HEADS * A_DV
A_CONV_CH = 2 * A_KD + A_VD
A_IN = A_CONV_CH + A_VD + 2 * A_HEADS
CONV_W = 4
CHUNK = 64
B_Q_HEADS = 16
B_KV_HEADS = 4
B_HEAD_DIM = 64
B_GROUP = B_Q_HEADS // B_KV_HEADS
WINDOW = 128
FFN_HIDDEN = -(-8 * D_MODEL // (3 * 256)) * 256
PLE_DIM = 256
EPS = 1e-6
L2_EPS = 1e-6
NEG_INF = -1e30

kernel_name = 'yoco_gated_delta_swa_sink_step'


def rmsnorm(x, g):
    x32 = x.astype(jnp.float32)
    y = x32 * lax.rsqrt(jnp.mean(x32 * x32, axis=-1, keepdims=True) + EPS)
    return (y * g.astype(jnp.float32)).astype(x.dtype)


def l2norm(x):
    x32 = x.astype(jnp.float32)
    return x32 * lax.rsqrt(jnp.sum(x32 * x32, axis=-1, keepdims=True) + L2_EPS)


def chunk_gated_delta(q, k, v, g, beta, s0):
    bsz, seqlen, nh, _ = q.shape
    dv = v.shape[-1]
    c = min(CHUNK, seqlen)
    n = -(-seqlen // c)
    pad = n * c - seqlen

    def prep(t):
        t = jnp.swapaxes(t, 1, 2)
        widths = [(0, 0)] * t.ndim
        widths[2] = (0, pad)
        t = jnp.pad(t, widths)
        return t.reshape(t.shape[:2] + (n, c) + t.shape[3:])

    q, k, v, g, beta = [prep(t) for t in (q, k, v, g, beta)]
    gc = jnp.cumsum(g, axis=-1)
    idx = jnp.arange(c)
    incl = idx[:, None] >= idx[None, :]
    strict = idx[:, None] > idx[None, :]
    diff = gc[..., :, None] - gc[..., None, :]
    decay = jnp.where(incl, jnp.exp(jnp.where(incl, diff, 0.0)), 0.0)
    kb = k * beta[..., None]
    a_mat = jnp.where(strict, jnp.einsum('bhnid,bhnjd->bhnij', kb, k) * decay, 0.0)
    eye = jnp.eye(c, dtype=jnp.float32)
    rhs = jnp.concatenate([v * beta[..., None], kb * jnp.exp(gc)[..., None]], axis=-1)
    sol = lax.linalg.triangular_solve(a_mat + eye, rhs, left_side=True, lower=True, unit_diagonal=True)
    u_base, w_dec = sol[..., :dv], sol[..., dv:]
    qk = jnp.where(incl, jnp.einsum('bhnid,bhnjd->bhnij', q, k) * decay, 0.0)
    q_dec = q * jnp.exp(gc)[..., None]
    k_dec = k * jnp.exp(gc[..., -1:] - gc)[..., None]
    g_last = jnp.exp(gc[..., -1])
    xs = tuple(jnp.moveaxis(t, 2, 0) for t in (u_base, w_dec, qk, q_dec, k_dec, g_last))

    def step(s, inp):
        ub, wd, qkc, qd, kd, gl = inp
        u = ub - jnp.einsum('bhck,bhkv->bhcv', wd, s)
        o = jnp.einsum('bhck,bhkv->bhcv', qd, s) + jnp.einsum('bhij,bhjv->bhiv', qkc, u)
        s = s * gl[..., None, None] + jnp.einsum('bhck,bhcv->bhkv', kd, u)
        return s, o

    s_fin, o = lax.scan(step, s0, xs)
    o = jnp.moveaxis(o, 0, 2).reshape(bsz, nh, n * c, dv)[:, :, :seqlen]
    return jnp.swapaxes(o, 1, 2), s_fin


def gated_delta_mixer(h, s0, conv_buf, norm_g, w_in, conv_w, a_log, dt_bias, out_g, w_out):
    bsz, seqlen, _ = h.shape
    proj = rmsnorm(h, norm_g) @ w_in
    qkv, z, a, b = jnp.split(proj, [A_CONV_CH, A_CONV_CH + A_VD, A_CONV_CH + A_VD + A_HEADS], axis=-1)
    full = jnp.concatenate([conv_buf.astype(qkv.dtype), qkv], axis=1)
    conv = full[:, 0:seqlen] * conv_w[0]
    for w in range(1, CONV_W):
        conv = conv + full[:, w:w + seqlen] * conv_w[w]
    new_buf = full[:, seqlen:]
    qkv = jax.nn.silu(conv)
    q, k, v = jnp.split(qkv, [A_KD, 2 * A_KD], axis=-1)
    q = l2norm(q.reshape(bsz, seqlen, A_HEADS, A_DK)) * (A_DK ** -0.5)
    k = l2norm(k.reshape(bsz, seqlen, A_HEADS, A_DK))
    v = v.reshape(bsz, seqlen, A_HEADS, A_DV).astype(jnp.float32)
    beta = jax.nn.sigmoid(b.astype(jnp.float32))
    g = -jnp.exp(a_log.astype(jnp.float32)) * jax.nn.softplus(a.astype(jnp.float32) + dt_bias.astype(jnp.float32))
    o, s_new = chunk_gated_delta(q, k, v, g, beta, s0.astype(jnp.float32))
    zf = jax.nn.silu(z.reshape(bsz, seqlen, A_HEADS, A_DV).astype(jnp.float32))
    o = o * lax.rsqrt(jnp.mean(o * o, axis=-1, keepdims=True) + EPS) * out_g.astype(jnp.float32) * zf
    out = o.reshape(bsz, seqlen, A_VD).astype(h.dtype) @ w_out
    return out, s_new.astype(s0.dtype), new_buf.astype(conv_buf.dtype)


def sink_attention(q, k, v, mask, sinks):
    s = jnp.einsum('...qhgd,...khd->...hgqk', q, k).astype(jnp.float32)
    s = jnp.where(mask, s, NEG_INF)
    sk = sinks.astype(jnp.float32).reshape(B_KV_HEADS, B_GROUP, 1, 1)
    m = jnp.maximum(jnp.max(s, axis=-1, keepdims=True), sk)
    p = jnp.exp(s - m)
    prob = p / (jnp.sum(p, axis=-1, keepdims=True) + jnp.exp(sk - m))
    return jnp.einsum('...hgqk,...khd->...qhgd', prob.astype(v.dtype), v)


def window_attention_prompt(q, k, v, sinks):
    bsz, seqlen = q.shape[:2]
    nb = seqlen // WINDOW
    qb = q.reshape(bsz, nb, WINDOW, B_KV_HEADS, B_GROUP, B_HEAD_DIM)

    def band(t):
        tb = t.reshape(bsz, nb, WINDOW, B_KV_HEADS, B_HEAD_DIM)
        prev = jnp.pad(tb[:, :-1], ((0, 0), (1, 0), (0, 0), (0, 0), (0, 0)))
        return jnp.concatenate([prev, tb], axis=2)

    qi = jnp.arange(WINDOW)[:, None]
    ki = jnp.arange(2 * WINDOW)[None, :]
    rel = qi + WINDOW - ki
    kpos = (jnp.arange(nb)[:, None, None] - 1) * WINDOW + ki
    mask = (rel >= 0) & (rel < WINDOW) & (kpos >= 0)
    o = sink_attention(qb, band(k), band(v), mask[:, None, None], sinks)
    return o.reshape(bsz, seqlen, B_Q_HEADS * B_HEAD_DIM)


def window_attention_sample(q, k_new, v_new, k_buf, v_buf, sinks):
    bsz, t = q.shape[:2]
    nbuf = k_buf.shape[1]
    kk = jnp.concatenate([k_buf.astype(k_new.dtype), k_new], axis=1)
    vv = jnp.concatenate([v_buf.astype(v_new.dtype), v_new], axis=1)
    qpos = PAST_LEN + jnp.arange(t)
    kpos = jnp.concatenate([PAST_LEN - nbuf + jnp.arange(nbuf), PAST_LEN + jnp.arange(t)])
    rel = qpos[:, None] - kpos[None, :]
    mask = (rel >= 0) & (rel < WINDOW)
    o = sink_attention(q, kk, vv, mask, sinks)
    return o.reshape(bsz, t, B_Q_HEADS * B_HEAD_DIM)


def swiglu(h, g, w_gu, w_down):
    gt, up = jnp.split(rmsnorm(h, g) @ w_gu, 2, axis=-1)
    return (jax.nn.silu(gt) * up) @ w_down


def ple_term(h, pe, g, w_proj, w_gate):
    gate = jax.nn.sigmoid(rmsnorm(h, g) @ w_gate)
    return (pe.astype(h.dtype) @ w_proj) * gate


def run_trunk(x, pe, s_init, c_init, k_buf, v_buf, win_buf, prm):
    bsz, seqlen, _ = x.shape
    h = x
    s_out, c_out = [], []
    k_sh = None
    v_sh = None
    for i in range(DEPTH):
        if i < N_A_LAYERS:
            mix, s_new, c_new = gated_delta_mixer(
                h, s_init[i], c_init[i], prm['a_norm'][i], prm['a_w_in'][i], prm['a_conv_w'][i],
                prm['a_a_log'][i], prm['a_dt_bias'][i], prm['a_out_norm'][i], prm['a_w_out'][i])
            s_out.append(s_new)
            c_out.append(c_new)
        else:
            if i == N_A_LAYERS:
                kv = (rmsnorm(h, prm['kv_norm']) @ prm['kv_w']).reshape(bsz, seqlen, 2, B_KV_HEADS, B_HEAD_DIM)
                k_sh = kv[:, :, 0]
                v_sh = kv[:, :, 1]
            j = i - N_A_LAYERS
            q = (rmsnorm(h, prm['b_norm'][j]) @ prm['b_w_q'][j]).reshape(
                bsz, seqlen, B_KV_HEADS, B_GROUP, B_HEAD_DIM) * (B_HEAD_DIM ** -0.5)
            if k_buf is None:
                o = window_attention_prompt(q, k_sh, v_sh, prm['b_sinks'][j])
            else:
                o = window_attention_sample(q, k_sh, v_sh, k_buf, v_buf, prm['b_sinks'][j])
            mix = o @ prm['b_w_o'][j]
        h = h + mix
        h = h + swiglu(h, prm['ffn_norm'][i], prm['ffn_w_gu'][i], prm['ffn_w_down'][i])
        h = h + ple_term(h, pe[i], prm['ple_norm'][i], prm['ple_w_proj'][i], prm['ple_w_gate'][i])
    y = rmsnorm(h, prm['final_norm'])
    if k_buf is None:
        k_win = k_sh[:, -win_buf:]
        v_win = v_sh[:, -win_buf:]
    else:
        k_win = jnp.concatenate([k_buf.astype(k_sh.dtype), k_sh], axis=1)[:, -win_buf:]
        v_win = jnp.concatenate([v_buf.astype(v_sh.dtype), v_sh], axis=1)[:, -win_buf:]
    return y, jnp.stack(s_out), jnp.stack(c_out), k_win, v_win


def setup_inputs(seed: int = 0) -> dict:
    key = jax.random.key(seed)
    ks = jax.random.split(key, 32)
    D = D_MODEL
    win_buf = min(WINDOW, PAST_LEN)

    def nrm(i, shape, scale):
        return jax.random.normal(ks[i], shape, jnp.float32) * scale

    def gain(i, shape):
        return 1.0 + nrm(i, shape, 0.02)

    dt = jnp.exp(jax.random.uniform(ks[22], (N_A_LAYERS, A_HEADS), jnp.float32,
                                    minval=math.log(1e-3), maxval=math.log(1e-1)))
    return {
        'x_prompt': nrm(0, (BATCH, SEQ, D), 1.0),
        'x_sample': nrm(1, (DEC_BATCH, DEC_SEQ, D), 1.0),
        'state_delta': nrm(2, (N_A_LAYERS, DEC_BATCH, A_HEADS, A_DK, A_DV), 0.1),
        'state_conv': nrm(3, (N_A_LAYERS, DEC_BATCH, CONV_W - 1, A_CONV_CH), 1.0),
        'cache_k_win': nrm(4, (DEC_BATCH, win_buf, B_KV_HEADS, B_HEAD_DIM), 1.0),
        'cache_v_win': nrm(5, (DEC_BATCH, win_buf, B_KV_HEADS, B_HEAD_DIM), 1.0),
        'p_prompt': nrm(6, (DEPTH, BATCH, SEQ, PLE_DIM), 1.0),
        'p_sample': nrm(7, (DEPTH, DEC_BATCH, DEC_SEQ, PLE_DIM), 1.0),
        'a_norm': gain(8, (N_A_LAYERS, D)),
        'a_w_in': nrm(9, (N_A_LAYERS, D, A_IN), D ** -0.5),
        'a_conv_w': nrm(10, (N_A_LAYERS, CONV_W, A_CONV_CH), CONV_W ** -0.5),
        'a_a_log': jnp.log(jax.random.uniform(ks[11], (N_A_LAYERS, A_HEADS), jnp.float32, minval=1.0, maxval=16.0)),
        'a_dt_bias': dt + jnp.log(-jnp.expm1(-dt)),
        'a_out_norm': gain(12, (N_A_LAYERS, A_DV)),
        'a_w_out': nrm(13, (N_A_LAYERS, A_VD, D), A_VD ** -0.5),
        'kv_norm': gain(14, (D,)),
        'kv_w': nrm(15, (D, 2 * B_KV_HEADS * B_HEAD_DIM), D ** -0.5),
        'b_norm': gain(16, (N_B_LAYERS, D)),
        'b_w_q': nrm(17, (N_B_LAYERS, D, B_Q_HEADS * B_HEAD_DIM), D ** -0.5),
        'b_sinks': nrm(18, (N_B_LAYERS, B_Q_HEADS), 1.0),
        'b_w_o': nrm(19, (N_B_LAYERS, B_Q_HEADS * B_HEAD_DIM, D), (B_Q_HEADS * B_HEAD_DIM) ** -0.5),
        'ffn_norm': gain(20, (DEPTH, D)),
        'ffn_w_gu': nrm(21, (DEPTH, D, 2 * FFN_HIDDEN), D ** -0.5),
        'ffn_w_down': nrm(23, (DEPTH, FFN_HIDDEN, D), FFN_HIDDEN ** -0.5),
        'ple_norm': gain(24, (DEPTH, D)),
        'ple_w_proj': nrm(25, (DEPTH, PLE_DIM, D), PLE_DIM ** -0.5),
        'ple_w_gate': nrm(26, (DEPTH, D, D), D ** -0.5),
        'final_norm': gain(27, (D,)),
    }


def reference(x_prompt, x_sample, state_delta, state_conv, cache_k_win, cache_v_win, p_prompt, p_sample,
              a_norm, a_w_in, a_conv_w, a_a_log, a_dt_bias, a_out_norm, a_w_out,
              kv_norm, kv_w, b_norm, b_w_q, b_sinks, b_w_o,
              ffn_norm, ffn_w_gu, ffn_w_down, ple_norm, ple_w_proj, ple_w_gate, final_norm):
    prm = {
        'a_norm': a_norm, 'a_w_in': a_w_in, 'a_conv_w': a_conv_w, 'a_a_log': a_a_log,
        'a_dt_bias': a_dt_bias, 'a_out_norm': a_out_norm, 'a_w_out': a_w_out,
        'kv_norm': kv_norm, 'kv_w': kv_w, 'b_norm': b_norm, 'b_w_q': b_w_q,
        'b_sinks': b_sinks, 'b_w_o': b_w_o, 'ffn_norm': ffn_norm, 'ffn_w_gu': ffn_w_gu,
        'ffn_w_down': ffn_w_down, 'ple_norm': ple_norm, 'ple_w_proj': ple_w_proj,
        'ple_w_gate': ple_w_gate, 'final_norm': final_norm,
    }
    win_buf = cache_k_win.shape[1]
    bsz = x_prompt.shape[0]
    s0 = jnp.zeros((N_A_LAYERS, bsz, A_HEADS, A_DK, A_DV), jnp.float32)
    c0 = jnp.zeros((N_A_LAYERS, bsz, CONV_W - 1, A_CONV_CH), x_prompt.dtype)
    y_prompt, sd_p, sc_p, kw_p, vw_p = run_trunk(x_prompt, p_prompt, s0, c0, None, None, win_buf, prm)
    y_sample, sd_s, sc_s, kw_s, vw_s = run_trunk(x_sample, p_sample, state_delta, state_conv,
                                                 cache_k_win, cache_v_win, win_buf, prm)
    return (y_prompt, y_sample, sd_p, sd_s, sc_p, sc_s, kw_p, kw_s, vw_p, vw_s)
```

```python
import functools

import jax
import jax.numpy as jnp
from jax import lax
from jax.experimental import pallas as pl
from jax.experimental.pallas import tpu as pltpu

F32 = jnp.float32
BF16 = jnp.bfloat16

D_MODEL = 1024
A_HEADS = 8
A_DK = 128
A_DV = 128
A_KD = A_HEADS * A_DK
A_VD = A_HEADS * A_DV
A_CONV_CH = 2 * A_KD + A_VD
CONV_W = 4
CHUNK = 64
B_Q_HEADS = 16
B_KV_HEADS = 4
B_HEAD_DIM = 64
B_GROUP = B_Q_HEADS // B_KV_HEADS
B_KVD = B_KV_HEADS * B_HEAD_DIM
WINDOW = 128
FFN_HIDDEN = 2816
PLE_DIM = 256
EPS = 1e-6
L2_EPS = 1e-6
NEG_INF = -1e30

LANES = 128
SUBLANES = 8
VMEM_LIMIT_BYTES = 56 * 1024 * 1024


def _rms(x, g):
    return x * lax.rsqrt(jnp.mean(x * x, axis=-1, keepdims=True) + EPS) * g


def _sigmoid(x):
    return 1.0 / (1.0 + jnp.exp(-x))


def _silu(x):
    return x * _sigmoid(x)


def _softplus(x):
    return jnp.maximum(x, 0.0) + jnp.log1p(jnp.exp(-jnp.abs(x)))


def _dot(a, b):
    return jnp.dot(a, b, preferred_element_type=F32)


def _dot_nt(a, b):
    return lax.dot_general(a, b, (((1,), (1,)), ((), ())), preferred_element_type=F32)


def _dot_tn(a, b):
    return lax.dot_general(a, b, (((0,), (0,)), ((), ())), preferred_element_type=F32)


def _split3(x):
    p0 = x.astype(BF16)
    r = x - p0.astype(F32)
    p1 = r.astype(BF16)
    p2 = (r - p1.astype(F32)).astype(BF16)
    return p0, p1, p2


def _dot_hi(a, b):
    a0 = a.astype(BF16)
    a1 = (a - a0.astype(F32)).astype(BF16)
    b0 = b.astype(BF16)
    b1 = (b - b0.astype(F32)).astype(BF16)
    return _dot(a0, b0) + (_dot(a0, b1) + _dot(a1, b0))


def _const_spec(shape):
    n = len(shape)
    return pl.BlockSpec(shape, lambda *_: (0,) * n, pipeline_mode=pl.Buffered(1))


def _inproj_kernel(x_ref, g_ref, w1_ref, w2_ref, qkv_ref, z_ref, ab_ref):
    xn = _rms(x_ref[...], g_ref[...]).astype(BF16)
    p = _dot(xn, w1_ref[...])
    qkv_ref[...] = p[:, :A_CONV_CH]
    z_ref[...] = p[:, A_CONV_CH:].astype(z_ref.dtype)
    ab_ref[...] = _dot(xn, w2_ref[...])


def _inproj(x2d, norm_g, w_qkvz, w_ab, tm):
    m = x2d.shape[0]
    row = lambda n: pl.BlockSpec((tm, n), lambda i: (i, 0))
    return pl.pallas_call(
        _inproj_kernel,
        grid=(m // tm,),
        in_specs=[row(D_MODEL), _const_spec((1, D_MODEL)),
                  _const_spec(w_qkvz.shape), _const_spec(w_ab.shape)],
        out_specs=[row(A_CONV_CH), row(A_VD), row(2 * LANES)],
        out_shape=[jax.ShapeDtypeStruct((m, A_CONV_CH), F32),
                   jax.ShapeDtypeStruct((m, A_VD), F32),
                   jax.ShapeDtypeStruct((m, 2 * LANES), F32)],
        compiler_params=pltpu.CompilerParams(
            dimension_semantics=("arbitrary",), vmem_limit_bytes=VMEM_LIMIT_BYTES),
        name="inproj",
    )(x2d, norm_g, w_qkvz, w_ab)


def _solve_unit_lower(a, rhs, c):
    x = rhs - _dot_hi(a, rhs)
    p = a
    n = 2
    while n < c:
        p = _dot_hi(p, p)
        x = x + _dot_hi(p, x)
        n *= 2
    return x


def _delta_kernel(qkv_ref, z_ref, ab_ref, s0_ref, cb_ref, cw_ref, gp_ref, og_ref,
                  o_ref, s_ref, c_ref,
                  ext_scr, qkv_scr, g_scr, b_scr, o_scr, *, tl, c):
    t = pl.program_id(1)
    nt = pl.num_programs(1)
    pad = SUBLANES

    @pl.when(t == 0)
    def _():
        s_ref[...] = s0_ref[...]
        ext_scr[pad - (CONV_W - 1):pad, :] = cb_ref[0]

    x = qkv_ref[0]
    ext_scr[pad:pad + tl, :] = x
    conv = x * cw_ref[CONV_W - 1:CONV_W, :]
    for j in range(1, CONV_W):
        conv = conv + ext_scr[pad - j:pad - j + tl, :] * cw_ref[CONV_W - 1 - j:CONV_W - j, :]
    new_hist = x[tl - (CONV_W - 1):, :]
    ext_scr[pad - (CONV_W - 1):pad, :] = new_hist

    @pl.when(t == nt - 1)
    def _():
        c_ref[0] = new_hist

    act = _silu(conv)
    for h in range(A_HEADS):
        qs = slice(h * A_DK, (h + 1) * A_DK)
        ks = slice(A_KD + h * A_DK, A_KD + (h + 1) * A_DK)
        qh = act[:, qs]
        kh = act[:, ks]
        qkv_scr[:, qs] = qh * lax.rsqrt(jnp.sum(qh * qh, axis=-1, keepdims=True) + L2_EPS) * (A_DK ** -0.5)
        qkv_scr[:, ks] = kh * lax.rsqrt(jnp.sum(kh * kh, axis=-1, keepdims=True) + L2_EPS)
    qkv_scr[:, 2 * A_KD:] = act[:, 2 * A_KD:]

    ab = ab_ref[0]
    g_scr[...] = -jnp.exp(gp_ref[0:1, :]) * _softplus(ab[:, :LANES] + gp_ref[1:2, :])
    b_scr[...] = _sigmoid(ab[:, LANES:])

    ri = lax.broadcasted_iota(jnp.int32, (c, c), 0)
    ci = lax.broadcasted_iota(jnp.int32, (c, c), 1)
    incl = ri >= ci
    strict = ri > ci
    tri = incl.astype(F32).astype(BF16)

    def chunk_body(n, carry):
        r0 = pl.multiple_of(n * c, c)
        rows = pl.ds(r0, c)
        g = g_scr[rows, :]
        beta = b_scr[rows, :]
        g0, g1, g2 = _split3(g)
        gc = _dot(tri, g0) + (_dot(tri, g1) + _dot(tri, g2))
        gct = gc.T
        for h in range(A_HEADS):
            gcol = gc[:, h:h + 1]
            grow = gct[h:h + 1, :]
            glast = gc[c - 1:c, h:h + 1]
            bcol = beta[:, h:h + 1]
            decay = jnp.where(incl, jnp.exp(jnp.where(incl, gcol - grow, 0.0)), 0.0)
            q = qkv_scr[rows, h * A_DK:(h + 1) * A_DK]
            k = qkv_scr[rows, A_KD + h * A_DK:A_KD + (h + 1) * A_DK]
            v = qkv_scr[rows, 2 * A_KD + h * A_DV:2 * A_KD + (h + 1) * A_DV]
            kb = k * bcol
            egc = jnp.exp(gcol)
            k16 = k.astype(BF16)
            a_mat = jnp.where(strict, _dot_nt(kb.astype(BF16), k16) * decay, 0.0)
            qk = jnp.where(incl, _dot_nt(q.astype(BF16), k16) * decay, 0.0)
            rhs = jnp.concatenate([v * bcol, kb * egc], axis=1)
            sol = _solve_unit_lower(a_mat, rhs, c)
            u_base = sol[:, :A_DV]
            w_dec = sol[:, A_DV:]
            s = s_ref[0, h]
            s16 = s.astype(BF16)
            u = u_base - _dot(w_dec.astype(BF16), s16)
            u16 = u.astype(BF16)
            o = _dot((q * egc).astype(BF16), s16) + _dot(qk.astype(BF16), u16)
            k_dec = k * jnp.exp(glast - gcol)
            s_ref[0, h] = s * jnp.exp(glast) + _dot_tn(k_dec.astype(BF16), u16)
            o_scr[rows, h * A_DV:(h + 1) * A_DV] = o
        return carry

    lax.fori_loop(0, tl // c, chunk_body, 0)

    z = z_ref[0].astype(F32)
    for h in range(A_HEADS):
        vs = slice(h * A_DV, (h + 1) * A_DV)
        oh = o_scr[:, vs]
        oh = oh * lax.rsqrt(jnp.mean(oh * oh, axis=-1, keepdims=True) + EPS) * og_ref[...]
        o_ref[0, :, vs] = (oh * _silu(z[:, vs])).astype(o_ref.dtype)


def _delta(qkv, z, ab, s0, cbuf, conv_w, gate_prm, out_g, tl, c):
    b, l, _ = qkv.shape
    tok = lambda n: pl.BlockSpec((1, tl, n), lambda i, j: (i, j, 0))
    per_b = lambda shape: pl.BlockSpec((1,) + shape, lambda i, j: (i,) + (0,) * len(shape))
    state = (A_HEADS, A_DK, A_DV)
    hist = (CONV_W - 1, A_CONV_CH)
    return pl.pallas_call(
        functools.partial(_delta_kernel, tl=tl, c=c),
        grid=(b, l // tl),
        in_specs=[tok(A_CONV_CH), tok(A_VD), tok(2 * LANES), per_b(state), per_b(hist),
                  _const_spec(conv_w.shape), _const_spec(gate_prm.shape), _const_spec(out_g.shape)],
        out_specs=[tok(A_VD), per_b(state), per_b(hist)],
        out_shape=[jax.ShapeDtypeStruct((b, l, A_VD), BF16),
                   jax.ShapeDtypeStruct((b,) + state, F32),
                   jax.ShapeDtypeStruct((b,) + hist, F32)],
        scratch_shapes=[pltpu.VMEM((tl + SUBLANES, A_CONV_CH), F32),
                        pltpu.VMEM((tl, A_CONV_CH), F32),
                        pltpu.VMEM((tl, LANES), F32),
                        pltpu.VMEM((tl, LANES), F32),
                        pltpu.VMEM((tl, A_VD), F32)],
        compiler_params=pltpu.CompilerParams(
            dimension_semantics=("arbitrary", "arbitrary"), vmem_limit_bytes=VMEM_LIMIT_BYTES),
        name="delta",
    )(qkv, z, ab, s0, cbuf, conv_w, gate_prm, out_g)


def _post_body(mix_ref, res_ref, pe_ref, wmix_ref, fg_ref, wg_ref, wu_ref, wd_ref,
               pg_ref, wpg_ref, wpp_ref, n_chunks):
    h = res_ref[...] + _dot(mix_ref[...], wmix_ref[...])
    xn = _rms(h, fg_ref[...]).astype(BF16)
    fc = FFN_HIDDEN // n_chunks
    acc = h
    for i in range(n_chunks):
        cs = slice(i * fc, (i + 1) * fc)
        gt = _dot(xn, wg_ref[:, cs])
        up = _dot(xn, wu_ref[:, cs])
        acc = acc + _dot((_silu(gt) * up).astype(BF16), wd_ref[cs, :])
    h2 = acc
    gate = _sigmoid(_dot(_rms(h2, pg_ref[...]).astype(BF16), wpg_ref[...]))
    return h2 + _dot(pe_ref[...].astype(BF16), wpp_ref[...]) * gate


def _post_mid_kernel(mix_ref, res_ref, pe_ref, wmix_ref, fg_ref, wg_ref, wu_ref, wd_ref,
                     pg_ref, wpg_ref, wpp_ref, kvn_ref, kvw_ref, bn_ref, wq_ref,
                     h_ref, q_ref, kv_ref, *, n_chunks):
    h3 = _post_body(mix_ref, res_ref, pe_ref, wmix_ref, fg_ref, wg_ref, wu_ref, wd_ref,
                    pg_ref, wpg_ref, wpp_ref, n_chunks)
    h_ref[...] = h3
    kv_ref[...] = _dot(_rms(h3, kvn_ref[...]).astype(BF16), kvw_ref[...])
    q = _dot(_rms(h3, bn_ref[...]).astype(BF16), wq_ref[...]) * (B_HEAD_DIM ** -0.5)
    q_ref[...] = q.astype(q_ref.dtype)


def _post_last_kernel(mix_ref, res_ref, pe_ref, wmix_ref, fg_ref, wg_ref, wu_ref, wd_ref,
                      pg_ref, wpg_ref, wpp_ref, fn_ref, y_ref, *, n_chunks):
    h3 = _post_body(mix_ref, res_ref, pe_ref, wmix_ref, fg_ref, wg_ref, wu_ref, wd_ref,
                    pg_ref, wpg_ref, wpp_ref, n_chunks)
    y_ref[...] = _rms(h3, fn_ref[...])


def _post(mix, res, pe, common, tail, tm, last):
    m = res.shape[0]
    row = lambda n: pl.BlockSpec((tm, n), lambda i: (i, 0))
    weights = list(common) + list(tail)
    in_specs = [row(mix.shape[1]), row(D_MODEL), row(PLE_DIM)] + [_const_spec(w.shape) for w in weights]
    if last:
        kern = _post_last_kernel
        out_specs = [row(D_MODEL)]
        out_shape = [jax.ShapeDtypeStruct((m, D_MODEL), F32)]
    else:
        kern = _post_mid_kernel
        out_specs = [row(D_MODEL), row(B_Q_HEADS * B_HEAD_DIM), row(2 * B_KVD)]
        out_shape = [jax.ShapeDtypeStruct((m, D_MODEL), F32),
                     jax.ShapeDtypeStruct((m, B_Q_HEADS * B_HEAD_DIM), BF16),
                     jax.ShapeDtypeStruct((m, 2 * B_KVD), F32)]
    return pl.pallas_call(
        functools.partial(kern, n_chunks=2),
        grid=(m // tm,),
        in_specs=in_specs,
        out_specs=out_specs,
        out_shape=out_shape,
        compiler_params=pltpu.CompilerParams(
            dimension_semantics=("arbitrary",), vmem_limit_bytes=VMEM_LIMIT_BYTES),
        name="post_last" if last else "post_mid",
    )(mix, res, pe, *weights)


def _sink_attend(q_ref, keys, vals, mask, sink_ref, n_q):
    k16 = keys.astype(BF16)
    v16 = vals.astype(BF16)
    outs = []
    for j in range(B_KV_HEADS):
        kj = k16[:, j * B_HEAD_DIM:(j + 1) * B_HEAD_DIM]
        vj = v16[:, j * B_HEAD_DIM:(j + 1) * B_HEAD_DIM]
        for g in range(B_GROUP):
            hq = j * B_GROUP + g
            qh = q_ref[0, :, hq * B_HEAD_DIM:(hq + 1) * B_HEAD_DIM]
            s = jnp.where(mask, _dot_nt(qh, kj), NEG_INF)
            sk = sink_ref[hq]
            mx = jnp.maximum(jnp.max(s, axis=-1, keepdims=True), sk)
            p = jnp.exp(s - mx)
            den = jnp.sum(p, axis=-1, keepdims=True) + jnp.exp(sk - mx)
            outs.append(_dot(p.astype(BF16), vj) / den)
    return jnp.concatenate(outs, axis=1)


def _attn_prompt_kernel(sink_ref, q_ref, kvp_ref, kvc_ref, o_ref):
    i = pl.program_id(1)
    kv = jnp.concatenate([kvp_ref[0], kvc_ref[0]], axis=0)
    qi = lax.broadcasted_iota(jnp.int32, (WINDOW, 2 * WINDOW), 0)
    ki = lax.broadcasted_iota(jnp.int32, (WINDOW, 2 * WINDOW), 1)
    rel = qi + WINDOW - ki
    kpos = (i - 1) * WINDOW + ki
    mask = (rel >= 0) & (rel < WINDOW) & (kpos >= 0)
    o = _sink_attend(q_ref, kv[:, :B_KVD], kv[:, B_KVD:], mask, sink_ref, WINDOW)
    o_ref[0] = o.astype(o_ref.dtype)


def _attn_prompt(q, kv, sinks):
    b, l, _ = q.shape
    nb = l // WINDOW
    return pl.pallas_call(
        _attn_prompt_kernel,
        grid=(b, nb),
        in_specs=[pl.BlockSpec(memory_space=pltpu.SMEM),
                  pl.BlockSpec((1, WINDOW, B_Q_HEADS * B_HEAD_DIM), lambda i, j: (i, j, 0)),
                  pl.BlockSpec((1, WINDOW, 2 * B_KVD), lambda i, j: (i, jnp.maximum(j - 1, 0), 0)),
                  pl.BlockSpec((1, WINDOW, 2 * B_KVD), lambda i, j: (i, j, 0))],
        out_specs=pl.BlockSpec((1, WINDOW, B_Q_HEADS * B_HEAD_DIM), lambda i, j: (i, j, 0)),
        out_shape=jax.ShapeDtypeStruct((b, l, B_Q_HEADS * B_HEAD_DIM), BF16),
        compiler_params=pltpu.CompilerParams(
            dimension_semantics=("arbitrary", "arbitrary"), vmem_limit_bytes=VMEM_LIMIT_BYTES),
        name="attn_prompt",
    )(sinks, q, kv, kv)


def _attn_sample_kernel(sink_ref, q_ref, kv_ref, ck_ref, cv_ref, o_ref, *, t, nbuf):
    kv = kv_ref[0]
    keys = jnp.concatenate([ck_ref[0], kv[:, :B_KVD]], axis=0)
    vals = jnp.concatenate([cv_ref[0], kv[:, B_KVD:]], axis=0)
    qi = lax.broadcasted_iota(jnp.int32, (t, nbuf + t), 0)
    ki = lax.broadcasted_iota(jnp.int32, (t, nbuf + t), 1)
    rel = qi + nbuf - ki
    mask = (rel >= 0) & (rel < WINDOW)
    o = _sink_attend(q_ref, keys, vals, mask, sink_ref, t)
    o_ref[0] = o.astype(o_ref.dtype)


def _attn_sample(q, kv, ck, cv, sinks):
    b, t, _ = q.shape
    nbuf = ck.shape[1]
    per_b = lambda r, n: pl.BlockSpec((1, r, n), lambda i: (i, 0, 0))
    return pl.pallas_call(
        functools.partial(_attn_sample_kernel, t=t, nbuf=nbuf),
        grid=(b,),
        in_specs=[pl.BlockSpec(memory_space=pltpu.SMEM),
                  per_b(t, B_Q_HEADS * B_HEAD_DIM), per_b(t, 2 * B_KVD),
                  per_b(nbuf, B_KVD), per_b(nbuf, B_KVD)],
        out_specs=per_b(t, B_Q_HEADS * B_HEAD_DIM),
        out_shape=jax.ShapeDtypeStruct((b, t, B_Q_HEADS * B_HEAD_DIM), BF16),
        compiler_params=pltpu.CompilerParams(
            dimension_semantics=("arbitrary",), vmem_limit_bytes=VMEM_LIMIT_BYTES),
        name="attn_sample",
    )(sinks, q, kv, ck, cv)


def _pad_lanes(v):
    return jnp.pad(v, ((0, 0), (0, LANES - v.shape[1])))


def _prepare(p):
    w_in = p["a_w_in"][0]
    n_gate = A_CONV_CH + A_VD
    row = lambda v: v.reshape(1, -1).astype(F32)

    def post_common(i, w_mix):
        gu = p["ffn_w_gu"][i]
        return (w_mix.astype(BF16), row(p["ffn_norm"][i]),
                gu[:, :FFN_HIDDEN].astype(BF16), gu[:, FFN_HIDDEN:].astype(BF16),
                p["ffn_w_down"][i].astype(BF16), row(p["ple_norm"][i]),
                p["ple_w_gate"][i].astype(BF16), p["ple_w_proj"][i].astype(BF16))

    return {
        "a_norm": row(p["a_norm"][0]),
        "w_qkvz": w_in[:, :n_gate].astype(BF16),
        "w_ab": jnp.concatenate([_pad_lanes(w_in[:, n_gate:n_gate + A_HEADS]),
                                 _pad_lanes(w_in[:, n_gate + A_HEADS:])], axis=1).astype(BF16),
        "conv_w": p["a_conv_w"][0],
        "gate_prm": jnp.concatenate([_pad_lanes(row(p["a_a_log"][0])),
                                     _pad_lanes(row(p["a_dt_bias"][0]))], axis=0),
        "out_g": row(p["a_out_norm"][0]),
        "post0": post_common(0, p["a_w_out"][0]),
        "tail0": (row(p["kv_norm"]), p["kv_w"].astype(BF16), row(p["b_norm"][0]), p["b_w_q"][0].astype(BF16)),
        "post1": post_common(1, p["b_w_o"][0]),
        "tail1": (row(p["final_norm"]),),
        "sinks": p["b_sinks"][0].astype(F32),
    }


def _trunk(x, pe, s_init, c_init, k_buf, v_buf, w, tm, tl, c):
    b, l, _ = x.shape
    m = b * l
    x2d = x.reshape(m, D_MODEL)
    qkv, z, ab = _inproj(x2d, w["a_norm"], w["w_qkvz"], w["w_ab"], tm)
    o, s_new, c_new = _delta(qkv.reshape(b, l, -1), z.reshape(b, l, -1), ab.reshape(b, l, -1),
                             s_init, c_init, w["conv_w"], w["gate_prm"], w["out_g"], tl, c)
    h, q, kv = _post(o.reshape(m, -1), x2d, pe[0].reshape(m, PLE_DIM), w["post0"], w["tail0"], tm, False)
    q3 = q.reshape(b, l, -1)
    kv3 = kv.reshape(b, l, -1)
    k_sh = kv3[:, :, :B_KVD].reshape(b, l, B_KV_HEADS, B_HEAD_DIM)
    v_sh = kv3[:, :, B_KVD:].reshape(b, l, B_KV_HEADS, B_HEAD_DIM)
    if k_buf is None:
        att = _attn_prompt(q3, kv3, w["sinks"])
        k_win = k_sh[:, -WINDOW:]
        v_win = v_sh[:, -WINDOW:]
    else:
        nbuf = k_buf.shape[1]
        att = _attn_sample(q3, kv3, k_buf.reshape(b, nbuf, B_KVD), v_buf.reshape(b, nbuf, B_KVD), w["sinks"])
        k_win = jnp.concatenate([k_buf, k_sh], axis=1)[:, -nbuf:]
        v_win = jnp.concatenate([v_buf, v_sh], axis=1)[:, -nbuf:]
    (y,) = _post(att.reshape(m, -1), h, pe[1].reshape(m, PLE_DIM), w["post1"], w["tail1"], tm, True)
    return y.reshape(b, l, D_MODEL), s_new[None], c_new[None], k_win, v_win


def kernel(x_prompt, x_sample, state_delta, state_conv, cache_k_win, cache_v_win, p_prompt, p_sample, a_norm, a_w_in, a_conv_w, a_a_log, a_dt_bias, a_out_norm, a_w_out, kv_norm, kv_w, b_norm, b_w_q, b_sinks, b_w_o, ffn_norm, ffn_w_gu, ffn_w_down, ple_norm, ple_w_proj, ple_w_gate, final_norm):
    w = _prepare({
        "a_norm": a_norm, "a_w_in": a_w_in, "a_conv_w": a_conv_w, "a_a_log": a_a_log,
        "a_dt_bias": a_dt_bias, "a_out_norm": a_out_norm, "a_w_out": a_w_out,
        "kv_norm": kv_norm, "kv_w": kv_w, "b_norm": b_norm, "b_w_q": b_w_q, "b_sinks": b_sinks,
        "b_w_o": b_w_o, "ffn_norm": ffn_norm, "ffn_w_gu": ffn_w_gu, "ffn_w_down": ffn_w_down,
        "ple_norm": ple_norm, "ple_w_proj": ple_w_proj, "ple_w_gate": ple_w_gate,
        "final_norm": final_norm,
    })
    bp, lp, _ = x_prompt.shape
    s0 = jnp.zeros((bp, A_HEADS, A_DK, A_DV), F32)
    c0 = jnp.zeros((bp, CONV_W - 1, A_CONV_CH), x_prompt.dtype)
    tm_p = min(512, bp * lp)
    tl_p = min(256, lp)
    y_p, sd_p, sc_p, kw_p, vw_p = _trunk(x_prompt, p_prompt, s0, c0, None, None, w,
                                         tm_p, tl_p, min(CHUNK, lp))
    bs, ls, _ = x_sample.shape
    tm_s = min(512, bs * ls)
    y_s, sd_s, sc_s, kw_s, vw_s = _trunk(x_sample, p_sample, state_delta[0], state_conv[0],
                                         cache_k_win, cache_v_win, w, tm_s, ls, min(CHUNK, ls))
    return (y_p, y_s, sd_p, sd_s, sc_p, sc_s, kw_p, kw_s, vw_p, vw_s)
```

```python
import functools

import jax
import jax.numpy as jnp
from jax import lax
from jax.experimental import pallas as pl
from jax.experimental.pallas import tpu as pltpu

F32 = jnp.float32
BF16 = jnp.bfloat16

D_MODEL = 1024
A_HEADS = 8
A_DK = 128
A_DV = 128
A_KD = A_HEADS * A_DK
A_VD = A_HEADS * A_DV
A_CONV_CH = 2 * A_KD + A_VD
CONV_W = 4
CHUNK = 64
B_Q_HEADS = 16
B_KV_HEADS = 4
B_HEAD_DIM = 64
B_GROUP = B_Q_HEADS // B_KV_HEADS
B_KVD = B_KV_HEADS * B_HEAD_DIM
WINDOW = 128
FFN_HIDDEN = 2816
PLE_DIM = 256
EPS = 1e-6
L2_EPS = 1e-6
NEG_INF = -1e30

LANES = 128
SUBLANES = 8
VMEM_LIMIT_BYTES = 56 * 1024 * 1024


def _rms(x, g):
    return x * lax.rsqrt(jnp.mean(x * x, axis=-1, keepdims=True) + EPS) * g


def _sigmoid(x):
    return 1.0 / (1.0 + jnp.exp(-x))


def _silu(x):
    return x * _sigmoid(x)


def _softplus(x):
    return jnp.maximum(x, 0.0) + jnp.log1p(jnp.exp(-jnp.abs(x)))


def _dot(a, b):
    return jnp.dot(a, b, preferred_element_type=F32)


def _dot_nt(a, b):
    return lax.dot_general(a, b, (((1,), (1,)), ((), ())), preferred_element_type=F32)


def _dot_tn(a, b):
    return lax.dot_general(a, b, (((0,), (0,)), ((), ())), preferred_element_type=F32)


def _split3(x):
    p0 = x.astype(BF16)
    r = x - p0.astype(F32)
    p1 = r.astype(BF16)
    p2 = (r - p1.astype(F32)).astype(BF16)
    return p0, p1, p2


def _const_spec(shape):
    n = len(shape)
    return pl.BlockSpec(shape, lambda *_: (0,) * n, pipeline_mode=pl.Buffered(1))


def _inproj_kernel(x_ref, g_ref, w1_ref, w2_ref, qkv_ref, z_ref, ab_ref):
    xn = _rms(x_ref[...], g_ref[...]).astype(BF16)
    p = _dot(xn, w1_ref[...])
    qkv_ref[...] = p[:, :A_CONV_CH]
    z_ref[...] = p[:, A_CONV_CH:].astype(z_ref.dtype)
    ab_ref[...] = _dot(xn, w2_ref[...])


def _inproj(x2d, norm_g, w_qkvz, w_ab, tm):
    m = x2d.shape[0]
    row = lambda n: pl.BlockSpec((tm, n), lambda i: (i, 0))
    return pl.pallas_call(
        _inproj_kernel,
        grid=(m // tm,),
        in_specs=[row(D_MODEL), _const_spec((1, D_MODEL)),
                  _const_spec(w_qkvz.shape), _const_spec(w_ab.shape)],
        out_specs=[row(A_CONV_CH), row(A_VD), row(2 * LANES)],
        out_shape=[jax.ShapeDtypeStruct((m, A_CONV_CH), F32),
                   jax.ShapeDtypeStruct((m, A_VD), F32),
                   jax.ShapeDtypeStruct((m, 2 * LANES), F32)],
        compiler_params=pltpu.CompilerParams(
            dimension_semantics=("arbitrary",), vmem_limit_bytes=VMEM_LIMIT_BYTES),
        name="inproj",
    )(x2d, norm_g, w_qkvz, w_ab)


def _hi_lo(x):
    hi = x.astype(BF16)
    return hi, (x - hi.astype(F32)).astype(BF16)


def _dot_split(a, b):
    return _dot(a[0], b[0]) + (_dot(a[0], b[1]) + _dot(a[1], b[0]))


def _solve_unit_lower(a, rhs, c):
    heads = range(len(a))
    p = [_hi_lo(a[h]) for h in heads]
    x = [rhs[h] - _dot_split(p[h], _hi_lo(rhs[h])) for h in heads]
    n = 2
    while n < c:
        p = [_hi_lo(_dot_split(p[h], p[h])) for h in heads]
        x = [x[h] + _dot_split(p[h], _hi_lo(x[h])) for h in heads]
        n *= 2
    return x


def _delta_kernel(qkv_ref, z_ref, ab_ref, s0_ref, cb_ref, cw_ref, gp_ref, og_ref,
                  o_ref, s_ref, c_ref,
                  ext_scr, qkv_scr, g_scr, b_scr, o_scr, *, tl, c):
    t = pl.program_id(1)
    nt = pl.num_programs(1)
    pad = SUBLANES

    @pl.when(t == 0)
    def _():
        s_ref[...] = s0_ref[...]
        ext_scr[pad - (CONV_W - 1):pad, :] = cb_ref[0]

    x = qkv_ref[0]
    ext_scr[pad:pad + tl, :] = x
    conv = x * cw_ref[CONV_W - 1:CONV_W, :]
    for j in range(1, CONV_W):
        conv = conv + ext_scr[pad - j:pad - j + tl, :] * cw_ref[CONV_W - 1 - j:CONV_W - j, :]
    new_hist = x[tl - (CONV_W - 1):, :]
    ext_scr[pad - (CONV_W - 1):pad, :] = new_hist

    @pl.when(t == nt - 1)
    def _():
        c_ref[0] = new_hist

    act = _silu(conv)
    for h in range(A_HEADS):
        qs = slice(h * A_DK, (h + 1) * A_DK)
        ks = slice(A_KD + h * A_DK, A_KD + (h + 1) * A_DK)
        qh = act[:, qs]
        kh = act[:, ks]
        qkv_scr[:, qs] = qh * lax.rsqrt(jnp.sum(qh * qh, axis=-1, keepdims=True) + L2_EPS) * (A_DK ** -0.5)
        qkv_scr[:, ks] = kh * lax.rsqrt(jnp.sum(kh * kh, axis=-1, keepdims=True) + L2_EPS)
    qkv_scr[:, 2 * A_KD:] = act[:, 2 * A_KD:]

    ab = ab_ref[0]
    g_scr[...] = -jnp.exp(gp_ref[0:1, :]) * _softplus(ab[:, :LANES] + gp_ref[1:2, :])
    b_scr[...] = _sigmoid(ab[:, LANES:])

    ri = lax.broadcasted_iota(jnp.int32, (c, c), 0)
    ci = lax.broadcasted_iota(jnp.int32, (c, c), 1)
    incl = ri >= ci
    strict = ri > ci
    tri = incl.astype(F32).astype(BF16)

    def chunk_body(n, carry):
        r0 = pl.multiple_of(n * c, c)
        rows = pl.ds(r0, c)
        g = g_scr[rows, :]
        beta = b_scr[rows, :]
        g0, g1, g2 = _split3(g)
        gc = _dot(tri, g0) + (_dot(tri, g1) + _dot(tri, g2))
        gct = gc.T
        heads = range(A_HEADS)
        gcol = [gc[:, h:h + 1] for h in heads]
        glast = [gc[c - 1:c, h:h + 1] for h in heads]
        bcol = [beta[:, h:h + 1] for h in heads]
        decay = [jnp.where(incl, jnp.exp(jnp.where(incl, gcol[h] - gct[h:h + 1, :], 0.0)), 0.0)
                 for h in heads]
        q = [qkv_scr[rows, h * A_DK:(h + 1) * A_DK] for h in heads]
        k = [qkv_scr[rows, A_KD + h * A_DK:A_KD + (h + 1) * A_DK] for h in heads]
        v = [qkv_scr[rows, 2 * A_KD + h * A_DV:2 * A_KD + (h + 1) * A_DV] for h in heads]
        kb = [k[h] * bcol[h] for h in heads]
        egc = [jnp.exp(gcol[h]) for h in heads]
        k16 = [k[h].astype(BF16) for h in heads]
        kk = [_dot_nt(kb[h].astype(BF16), k16[h]) for h in heads]
        qk = [_dot_nt(q[h].astype(BF16), k16[h]) for h in heads]
        a_mat = [jnp.where(strict, kk[h] * decay[h], 0.0) for h in heads]
        qk = [jnp.where(incl, qk[h] * decay[h], 0.0) for h in heads]
        rhs = [jnp.concatenate([v[h] * bcol[h], kb[h] * egc[h]], axis=1) for h in heads]
        sol = _solve_unit_lower(a_mat, rhs, c)
        s = [s_ref[0, h] for h in heads]
        s16 = [s[h].astype(BF16) for h in heads]
        ws = [_dot(sol[h][:, A_DV:].astype(BF16), s16[h]) for h in heads]
        qs = [_dot((q[h] * egc[h]).astype(BF16), s16[h]) for h in heads]
        u16 = [(sol[h][:, :A_DV] - ws[h]).astype(BF16) for h in heads]
        o = [qs[h] + _dot(qk[h].astype(BF16), u16[h]) for h in heads]
        ku = [_dot_tn((k[h] * jnp.exp(glast[h] - gcol[h])).astype(BF16), u16[h]) for h in heads]
        for h in heads:
            s_ref[0, h] = s[h] * jnp.exp(glast[h]) + ku[h]
            o_scr[rows, h * A_DV:(h + 1) * A_DV] = o[h]
        return carry

    lax.fori_loop(0, tl // c, chunk_body, 0)

    z = z_ref[0].astype(F32)
    for h in range(A_HEADS):
        vs = slice(h * A_DV, (h + 1) * A_DV)
        oh = o_scr[:, vs]
        oh = oh * lax.rsqrt(jnp.mean(oh * oh, axis=-1, keepdims=True) + EPS) * og_ref[...]
        o_ref[0, :, vs] = (oh * _silu(z[:, vs])).astype(o_ref.dtype)


def _delta(qkv, z, ab, s0, cbuf, conv_w, gate_prm, out_g, tl, c):
    b, l, _ = qkv.shape
    tok = lambda n: pl.BlockSpec((1, tl, n), lambda i, j: (i, j, 0))
    per_b = lambda shape: pl.BlockSpec((1,) + shape, lambda i, j: (i,) + (0,) * len(shape))
    state = (A_HEADS, A_DK, A_DV)
    hist = (CONV_W - 1, A_CONV_CH)
    return pl.pallas_call(
        functools.partial(_delta_kernel, tl=tl, c=c),
        grid=(b, l // tl),
        in_specs=[tok(A_CONV_CH), tok(A_VD), tok(2 * LANES), per_b(state), per_b(hist),
                  _const_spec(conv_w.shape), _const_spec(gate_prm.shape), _const_spec(out_g.shape)],
        out_specs=[tok(A_VD), per_b(state), per_b(hist)],
        out_shape=[jax.ShapeDtypeStruct((b, l, A_VD), BF16),
                   jax.ShapeDtypeStruct((b,) + state, F32),
                   jax.ShapeDtypeStruct((b,) + hist, F32)],
        scratch_shapes=[pltpu.VMEM((tl + SUBLANES, A_CONV_CH), F32),
                        pltpu.VMEM((tl, A_CONV_CH), F32),
                        pltpu.VMEM((tl, LANES), F32),
                        pltpu.VMEM((tl, LANES), F32),
                        pltpu.VMEM((tl, A_VD), F32)],
        compiler_params=pltpu.CompilerParams(
            dimension_semantics=("arbitrary", "arbitrary"), vmem_limit_bytes=VMEM_LIMIT_BYTES),
        name="delta",
    )(qkv, z, ab, s0, cbuf, conv_w, gate_prm, out_g)


def _post_body(mix_ref, res_ref, pe_ref, wmix_ref, fg_ref, wg_ref, wu_ref, wd_ref,
               pg_ref, wpg_ref, wpp_ref, n_chunks):
    h = res_ref[...] + _dot(mix_ref[...], wmix_ref[...])
    xn = _rms(h, fg_ref[...]).astype(BF16)
    fc = FFN_HIDDEN // n_chunks
    acc = h
    for i in range(n_chunks):
        cs = slice(i * fc, (i + 1) * fc)
        gt = _dot(xn, wg_ref[:, cs])
        up = _dot(xn, wu_ref[:, cs])
        acc = acc + _dot((_silu(gt) * up).astype(BF16), wd_ref[cs, :])
    h2 = acc
    gate = _sigmoid(_dot(_rms(h2, pg_ref[...]).astype(BF16), wpg_ref[...]))
    return h2 + _dot(pe_ref[...].astype(BF16), wpp_ref[...]) * gate


def _post_mid_kernel(mix_ref, res_ref, pe_ref, wmix_ref, fg_ref, wg_ref, wu_ref, wd_ref,
                     pg_ref, wpg_ref, wpp_ref, kvn_ref, kvw_ref, bn_ref, wq_ref,
                     h_ref, q_ref, kv_ref, *, n_chunks):
    h3 = _post_body(mix_ref, res_ref, pe_ref, wmix_ref, fg_ref, wg_ref, wu_ref, wd_ref,
                    pg_ref, wpg_ref, wpp_ref, n_chunks)
    h_ref[...] = h3
    kv_ref[...] = _dot(_rms(h3, kvn_ref[...]).astype(BF16), kvw_ref[...])
    q = _dot(_rms(h3, bn_ref[...]).astype(BF16), wq_ref[...]) * (B_HEAD_DIM ** -0.5)
    q_ref[...] = q.astype(q_ref.dtype)


def _post_last_kernel(mix_ref, res_ref, pe_ref, wmix_ref, fg_ref, wg_ref, wu_ref, wd_ref,
                      pg_ref, wpg_ref, wpp_ref, fn_ref, y_ref, *, n_chunks):
    h3 = _post_body(mix_ref, res_ref, pe_ref, wmix_ref, fg_ref, wg_ref, wu_ref, wd_ref,
                    pg_ref, wpg_ref, wpp_ref, n_chunks)
    y_ref[...] = _rms(h3, fn_ref[...])


def _post(mix, res, pe, common, tail, tm, last):
    m = res.shape[0]
    row = lambda n: pl.BlockSpec((tm, n), lambda i: (i, 0))
    weights = list(common) + list(tail)
    in_specs = [row(mix.shape[1]), row(D_MODEL), row(PLE_DIM)] + [_const_spec(w.shape) for w in weights]
    if last:
        kern = _post_last_kernel
        out_specs = [row(D_MODEL)]
        out_shape = [jax.ShapeDtypeStruct((m, D_MODEL), F32)]
    else:
        kern = _post_mid_kernel
        out_specs = [row(D_MODEL), row(B_Q_HEADS * B_HEAD_DIM), row(2 * B_KVD)]
        out_shape = [jax.ShapeDtypeStruct((m, D_MODEL), F32),
                     jax.ShapeDtypeStruct((m, B_Q_HEADS * B_HEAD_DIM), BF16),
                     jax.ShapeDtypeStruct((m, 2 * B_KVD), F32)]
    return pl.pallas_call(
        functools.partial(kern, n_chunks=2),
        grid=(m // tm,),
        in_specs=in_specs,
        out_specs=out_specs,
        out_shape=out_shape,
        compiler_params=pltpu.CompilerParams(
            dimension_semantics=("arbitrary",), vmem_limit_bytes=VMEM_LIMIT_BYTES),
        name="post_last" if last else "post_mid",
    )(mix, res, pe, *weights)


def _sink_attend(q_ref, keys, vals, mask, sink_ref, n_q):
    k16 = keys.astype(BF16)
    v16 = vals.astype(BF16)
    outs = []
    for j in range(B_KV_HEADS):
        kj = k16[:, j * B_HEAD_DIM:(j + 1) * B_HEAD_DIM]
        vj = v16[:, j * B_HEAD_DIM:(j + 1) * B_HEAD_DIM]
        for g in range(B_GROUP):
            hq = j * B_GROUP + g
            qh = q_ref[0, :, hq * B_HEAD_DIM:(hq + 1) * B_HEAD_DIM]
            s = jnp.where(mask, _dot_nt(qh, kj), NEG_INF)
            sk = sink_ref[hq]
            mx = jnp.maximum(jnp.max(s, axis=-1, keepdims=True), sk)
            p = jnp.exp(s - mx)
            den = jnp.sum(p, axis=-1, keepdims=True) + jnp.exp(sk - mx)
            outs.append(_dot(p.astype(BF16), vj) / den)
    return jnp.concatenate(outs, axis=1)


def _attn_prompt_kernel(sink_ref, q_ref, kvp_ref, kvc_ref, o_ref):
    i = pl.program_id(1)
    kv = jnp.concatenate([kvp_ref[0], kvc_ref[0]], axis=0)
    qi = lax.broadcasted_iota(jnp.int32, (WINDOW, 2 * WINDOW), 0)
    ki = lax.broadcasted_iota(jnp.int32, (WINDOW, 2 * WINDOW), 1)
    rel = qi + WINDOW - ki
    kpos = (i - 1) * WINDOW + ki
    mask = (rel >= 0) & (rel < WINDOW) & (kpos >= 0)
    o = _sink_attend(q_ref, kv[:, :B_KVD], kv[:, B_KVD:], mask, sink_ref, WINDOW)
    o_ref[0] = o.astype(o_ref.dtype)


def _attn_prompt(q, kv, sinks):
    b, l, _ = q.shape
    nb = l // WINDOW
    return pl.pallas_call(
        _attn_prompt_kernel,
        grid=(b, nb),
        in_specs=[pl.BlockSpec(memory_space=pltpu.SMEM),
                  pl.BlockSpec((1, WINDOW, B_Q_HEADS * B_HEAD_DIM), lambda i, j: (i, j, 0)),
                  pl.BlockSpec((1, WINDOW, 2 * B_KVD), lambda i, j: (i, jnp.maximum(j - 1, 0), 0)),
                  pl.BlockSpec((1, WINDOW, 2 * B_KVD), lambda i, j: (i, j, 0))],
        out_specs=pl.BlockSpec((1, WINDOW, B_Q_HEADS * B_HEAD_DIM), lambda i, j: (i, j, 0)),
        out_shape=jax.ShapeDtypeStruct((b, l, B_Q_HEADS * B_HEAD_DIM), BF16),
        compiler_params=pltpu.CompilerParams(
            dimension_semantics=("arbitrary", "arbitrary"), vmem_limit_bytes=VMEM_LIMIT_BYTES),
        name="attn_prompt",
    )(sinks, q, kv, kv)


def _attn_sample_kernel(sink_ref, q_ref, kv_ref, ck_ref, cv_ref, o_ref, *, t, nbuf):
    kv = kv_ref[0]
    keys = jnp.concatenate([ck_ref[0], kv[:, :B_KVD]], axis=0)
    vals = jnp.concatenate([cv_ref[0], kv[:, B_KVD:]], axis=0)
    qi = lax.broadcasted_iota(jnp.int32, (t, nbuf + t), 0)
    ki = lax.broadcasted_iota(jnp.int32, (t, nbuf + t), 1)
    rel = qi + nbuf - ki
    mask = (rel >= 0) & (rel < WINDOW)
    o = _sink_attend(q_ref, keys, vals, mask, sink_ref, t)
    o_ref[0] = o.astype(o_ref.dtype)


def _attn_sample(q, kv, ck, cv, sinks):
    b, t, _ = q.shape
    nbuf = ck.shape[1]
    per_b = lambda r, n: pl.BlockSpec((1, r, n), lambda i: (i, 0, 0))
    return pl.pallas_call(
        functools.partial(_attn_sample_kernel, t=t, nbuf=nbuf),
        grid=(b,),
        in_specs=[pl.BlockSpec(memory_space=pltpu.SMEM),
                  per_b(t, B_Q_HEADS * B_HEAD_DIM), per_b(t, 2 * B_KVD),
                  per_b(nbuf, B_KVD), per_b(nbuf, B_KVD)],
        out_specs=per_b(t, B_Q_HEADS * B_HEAD_DIM),
        out_shape=jax.ShapeDtypeStruct((b, t, B_Q_HEADS * B_HEAD_DIM), BF16),
        compiler_params=pltpu.CompilerParams(
            dimension_semantics=("arbitrary",), vmem_limit_bytes=VMEM_LIMIT_BYTES),
        name="attn_sample",
    )(sinks, q, kv, ck, cv)


def _pad_lanes(v):
    return jnp.pad(v, ((0, 0), (0, LANES - v.shape[1])))


def _prepare(p):
    w_in = p["a_w_in"][0]
    n_gate = A_CONV_CH + A_VD
    row = lambda v: v.reshape(1, -1).astype(F32)

    def post_common(i, w_mix):
        gu = p["ffn_w_gu"][i]
        return (w_mix.astype(BF16), row(p["ffn_norm"][i]),
                gu[:, :FFN_HIDDEN].astype(BF16), gu[:, FFN_HIDDEN:].astype(BF16),
                p["ffn_w_down"][i].astype(BF16), row(p["ple_norm"][i]),
                p["ple_w_gate"][i].astype(BF16), p["ple_w_proj"][i].astype(BF16))

    return {
        "a_norm": row(p["a_norm"][0]),
        "w_qkvz": w_in[:, :n_gate].astype(BF16),
        "w_ab": jnp.concatenate([_pad_lanes(w_in[:, n_gate:n_gate + A_HEADS]),
                                 _pad_lanes(w_in[:, n_gate + A_HEADS:])], axis=1).astype(BF16),
        "conv_w": p["a_conv_w"][0],
        "gate_prm": jnp.concatenate([_pad_lanes(row(p["a_a_log"][0])),
                                     _pad_lanes(row(p["a_dt_bias"][0]))], axis=0),
        "out_g": row(p["a_out_norm"][0]),
        "post0": post_common(0, p["a_w_out"][0]),
        "tail0": (row(p["kv_norm"]), p["kv_w"].astype(BF16), row(p["b_norm"][0]), p["b_w_q"][0].astype(BF16)),
        "post1": post_common(1, p["b_w_o"][0]),
        "tail1": (row(p["final_norm"]),),
        "sinks": p["b_sinks"][0].astype(F32),
    }


def _trunk(x, pe, s_init, c_init, k_buf, v_buf, w, tm, tl, c):
    b, l, _ = x.shape
    m = b * l
    x2d = x.reshape(m, D_MODEL)
    qkv, z, ab = _inproj(x2d, w["a_norm"], w["w_qkvz"], w["w_ab"], tm)
    o, s_new, c_new = _delta(qkv.reshape(b, l, -1), z.reshape(b, l, -1), ab.reshape(b, l, -1),
                             s_init, c_init, w["conv_w"], w["gate_prm"], w["out_g"], tl, c)
    h, q, kv = _post(o.reshape(m, -1), x2d, pe[0].reshape(m, PLE_DIM), w["post0"], w["tail0"], tm, False)
    q3 = q.reshape(b, l, -1)
    kv3 = kv.reshape(b, l, -1)
    k_sh = kv3[:, :, :B_KVD].reshape(b, l, B_KV_HEADS, B_HEAD_DIM)
    v_sh = kv3[:, :, B_KVD:].reshape(b, l, B_KV_HEADS, B_HEAD_DIM)
    if k_buf is None:
        att = _attn_prompt(q3, kv3, w["sinks"])
        k_win = k_sh[:, -WINDOW:]
        v_win = v_sh[:, -WINDOW:]
    else:
        nbuf = k_buf.shape[1]
        att = _attn_sample(q3, kv3, k_buf.reshape(b, nbuf, B_KVD), v_buf.reshape(b, nbuf, B_KVD), w["sinks"])
        k_win = jnp.concatenate([k_buf, k_sh], axis=1)[:, -nbuf:]
        v_win = jnp.concatenate([v_buf, v_sh], axis=1)[:, -nbuf:]
    (y,) = _post(att.reshape(m, -1), h, pe[1].reshape(m, PLE_DIM), w["post1"], w["tail1"], tm, True)
    return y.reshape(b, l, D_MODEL), s_new[None], c_new[None], k_win, v_win


def kernel(x_prompt, x_sample, state_delta, state_conv, cache_k_win, cache_v_win, p_prompt, p_sample, a_norm, a_w_in, a_conv_w, a_a_log, a_dt_bias, a_out_norm, a_w_out, kv_norm, kv_w, b_norm, b_w_q, b_sinks, b_w_o, ffn_norm, ffn_w_gu, ffn_w_down, ple_norm, ple_w_proj, ple_w_gate, final_norm):
    w = _prepare({
        "a_norm": a_norm, "a_w_in": a_w_in, "a_conv_w": a_conv_w, "a_a_log": a_a_log,
        "a_dt_bias": a_dt_bias, "a_out_norm": a_out_norm, "a_w_out": a_w_out,
        "kv_norm": kv_norm, "kv_w": kv_w, "b_norm": b_norm, "b_w_q": b_w_q, "b_sinks": b_sinks,
        "b_w_o": b_w_o, "ffn_norm": ffn_norm, "ffn_w_gu": ffn_w_gu, "ffn_w_down": ffn_w_down,
        "ple_norm": ple_norm, "ple_w_proj": ple_w_proj, "ple_w_gate": ple_w_gate,
        "final_norm": final_norm,
    })
    bp, lp, _ = x_prompt.shape
    s0 = jnp.zeros((bp, A_HEADS, A_DK, A_DV), F32)
    c0 = jnp.zeros((bp, CONV_W - 1, A_CONV_CH), x_prompt.dtype)
    tm_p = min(512, bp * lp)
    tl_p = min(256, lp)
    y_p, sd_p, sc_p, kw_p, vw_p = _trunk(x_prompt, p_prompt, s0, c0, None, None, w,
                                         tm_p, tl_p, min(CHUNK, lp))
    bs, ls, _ = x_sample.shape
    tm_s = min(512, bs * ls)
    y_s, sd_s, sc_s, kw_s, vw_s = _trunk(x_sample, p_sample, state_delta[0], state_conv[0],
                                         cache_k_win, cache_v_win, w, tm_s, ls, min(CHUNK, ls))
    return (y_p, y_s, sd_p, sd_s, sc_p, sc_s, kw_p, kw_s, vw_p, vw_s)
```

```python
import functools

import jax
import jax.numpy as jnp
from jax import lax
from jax.experimental import pallas as pl
from jax.experimental.pallas import tpu as pltpu

F32 = jnp.float32
BF16 = jnp.bfloat16

D_MODEL = 1024
A_HEADS = 8
A_DK = 128
A_DV = 128
A_KD = A_HEADS * A_DK
A_VD = A_HEADS * A_DV
A_CONV_CH = 2 * A_KD + A_VD
CONV_W = 4
CHUNK = 64
B_Q_HEADS = 16
B_KV_HEADS = 4
B_HEAD_DIM = 64
B_GROUP = B_Q_HEADS // B_KV_HEADS
B_QD = B_Q_HEADS * B_HEAD_DIM
B_KVD = B_KV_HEADS * B_HEAD_DIM
WINDOW = 128
FFN_HIDDEN = 2816
PLE_DIM = 256
EPS = 1e-6
L2_EPS = 1e-6
NEG_INF = -1e30

LANES = 128
SUBLANES = 8
VMEM_LIMIT_BYTES = 56 * 1024 * 1024
PROJ_SLAB = 512


def _rms(x, g):
    return x * lax.rsqrt(jnp.mean(x * x, axis=-1, keepdims=True) + EPS) * g


def _sigmoid(x):
    return 1.0 / (1.0 + jnp.exp(-x))


def _silu(x):
    return x * _sigmoid(x)


def _softplus(x):
    return jnp.maximum(x, 0.0) + jnp.log1p(jnp.exp(-jnp.abs(x)))


def _dot(a, b):
    return jnp.dot(a, b, preferred_element_type=F32)


def _dot_nt(a, b):
    return lax.dot_general(a, b, (((1,), (1,)), ((), ())), preferred_element_type=F32)


def _dot_tn(a, b):
    return lax.dot_general(a, b, (((0,), (0,)), ((), ())), preferred_element_type=F32)


def _hi_lo(x):
    hi = x.astype(BF16)
    return hi, (x - hi.astype(F32)).astype(BF16)


def _split3(x):
    p0 = x.astype(BF16)
    r = x - p0.astype(F32)
    p1 = r.astype(BF16)
    p2 = (r - p1.astype(F32)).astype(BF16)
    return p0, p1, p2


def _dot_split(a, b):
    return _dot(a[0], b[0]) + (_dot(a[0], b[1]) + _dot(a[1], b[0]))


def _const_spec(shape):
    n = len(shape)
    return pl.BlockSpec(shape, lambda *_: (0,) * n, pipeline_mode=pl.Buffered(1))


def _params(n_axes):
    return pltpu.CompilerParams(dimension_semantics=("arbitrary",) * n_axes,
                                vmem_limit_bytes=VMEM_LIMIT_BYTES)


def _proj_conv_kernel(x_ref, cb_ref, ng_ref, wqkv_ref, wz_ref, wab_ref, cw_ref, gp_ref,
                      qkv_ref, zg_ref, gb_ref, c_ref, ext_scr, *, bb, tl):
    hist = CONV_W - 1
    pad = SUBLANES
    m = bb * tl

    @pl.when(pl.program_id(1) == 0)
    def _():
        ext_scr[:, pad - hist:pad, :] = cb_ref[...]

    xn = _rms(x_ref[...].reshape(m, D_MODEL), ng_ref[...]).astype(BF16)

    for s in range(A_CONV_CH // PROJ_SLAB):
        c0 = s * PROJ_SLAB
        cols = slice(c0, c0 + PROJ_SLAB)
        p = _dot(xn, wqkv_ref[:, cols]).reshape(bb, tl, PROJ_SLAB)
        ext_scr[:, pad:pad + tl, cols] = p
        conv = p * cw_ref[hist:hist + 1, cols]
        for j in range(1, CONV_W):
            conv = conv + ext_scr[:, pad - j:pad - j + tl, cols] * cw_ref[hist - j:hist - j + 1, cols]
        tail = p[:, tl - hist:, :]
        ext_scr[:, pad - hist:pad, cols] = tail
        c_ref[:, :, cols] = tail
        act = _silu(conv)
        if c0 < 2 * A_KD:
            scale = A_DK ** -0.5 if c0 < A_KD else 1.0
            for h in range(PROJ_SLAB // A_DK):
                a = act[:, :, h * A_DK:(h + 1) * A_DK]
                a = a * lax.rsqrt(jnp.sum(a * a, axis=-1, keepdims=True) + L2_EPS) * scale
                qkv_ref[:, :, c0 + h * A_DK:c0 + (h + 1) * A_DK] = a.astype(qkv_ref.dtype)
        else:
            qkv_ref[:, :, cols] = act.astype(qkv_ref.dtype)

    for s in range(A_VD // PROJ_SLAB):
        cols = slice(s * PROJ_SLAB, (s + 1) * PROJ_SLAB)
        zg = _silu(_dot(xn, wz_ref[:, cols]))
        zg_ref[:, :, cols] = zg.reshape(bb, tl, PROJ_SLAB).astype(zg_ref.dtype)

    ab = _dot(xn, wab_ref[...])
    g = -jnp.exp(gp_ref[0:1, :]) * _softplus(ab[:, :LANES] + gp_ref[1:2, :])
    beta = _sigmoid(ab[:, LANES:])
    gb_ref[...] = jnp.concatenate([g, beta], axis=1).reshape(bb, tl, 2 * LANES)


def _proj_conv(x, cbuf, w, bb, tl, act_dtype):
    b, l, _ = x.shape
    tok = lambda n: pl.BlockSpec((bb, tl, n), lambda i, j: (i, j, 0))
    hist = pl.BlockSpec((bb, CONV_W - 1, A_CONV_CH), lambda i, j: (i, 0, 0))
    consts = [w["a_norm"], w["w_qkv"], w["w_z"], w["w_ab"], w["conv_w"], w["gate_prm"]]
    return pl.pallas_call(
        functools.partial(_proj_conv_kernel, bb=bb, tl=tl),
        grid=(b // bb, l // tl),
        in_specs=[tok(D_MODEL), hist] + [_const_spec(c.shape) for c in consts],
        out_specs=[tok(A_CONV_CH), tok(A_VD), tok(2 * LANES), hist],
        out_shape=[jax.ShapeDtypeStruct((b, l, A_CONV_CH), act_dtype),
                   jax.ShapeDtypeStruct((b, l, A_VD), act_dtype),
                   jax.ShapeDtypeStruct((b, l, 2 * LANES), F32),
                   jax.ShapeDtypeStruct((b, CONV_W - 1, A_CONV_CH), F32)],
        scratch_shapes=[pltpu.VMEM((bb, tl + SUBLANES, A_CONV_CH), F32)],
        compiler_params=_params(2),
        name="proj_conv",
    )(x, cbuf, *consts)


def _out_gate(o_scr, zg_ref, og_ref, o_ref):
    for h in range(A_HEADS):
        vs = slice(h * A_DV, (h + 1) * A_DV)
        oh = o_scr[:, vs]
        oh = oh * lax.rsqrt(jnp.mean(oh * oh, axis=-1, keepdims=True) + EPS) * og_ref[...]
        o_ref[0, :, vs] = (oh * zg_ref[0, :, vs].astype(F32)).astype(o_ref.dtype)


def _delta_packed_kernel(qkv_ref, zg_ref, gb_ref, s0_ref, og_ref, o_ref, s_ref,
                         wq_scr, ub_scr, kd_scr, qk_scr, o_scr, *, tl, c):
    gp = LANES // c
    npk = A_HEADS // gp
    pw = gp * c
    nc = tl // c

    @pl.when(pl.program_id(1) == 0)
    def _():
        s_ref[...] = s0_ref[...]

    ri = lax.broadcasted_iota(jnp.int32, (c, pw), 0)
    li = lax.broadcasted_iota(jnp.int32, (c, pw), 1)
    cj = li % c
    blk = li // c
    incl = ri >= cj
    strict = ri > cj
    eye = (ri == cj).astype(F32)
    tri = (lax.broadcasted_iota(jnp.int32, (c, c), 0) >= lax.broadcasted_iota(jnp.int32, (c, c), 1))
    tri = tri.astype(F32).astype(BF16)
    wide_blk = lax.broadcasted_iota(jnp.int32, (c, gp * A_DK), 1) // A_DK

    def lanes_of(cols, width):
        sel = blk if width == pw else wide_blk
        out = jnp.broadcast_to(cols[-1], (c, width))
        for i in range(gp - 2, -1, -1):
            out = jnp.where(sel == i, jnp.broadcast_to(cols[i], (c, width)), out)
        return out

    def block_diag(y):
        return jnp.concatenate([jnp.where(blk == i, y, 0.0) for i in range(gp)], axis=0).astype(BF16)

    def block_rows(parts):
        w = parts[0].shape[1]
        z = jnp.zeros((c, w), parts[0].dtype)
        return jnp.concatenate(
            [jnp.concatenate([parts[i] if j == i else z for j in range(gp)], axis=1) for i in range(gp)],
            axis=0)

    items = [(n, p) for n in range(nc) for p in range(npk)]
    gc = []
    for n in range(nc):
        g0, g1, g2 = _split3(gb_ref[0, n * c:(n + 1) * c, :LANES])
        gc.append(_dot(tri, g0) + (_dot(tri, g1) + _dot(tri, g2)))
    decay, a_neg, t_mat = {}, {}, {}
    for n, p in items:
        rows = slice(n * c, (n + 1) * c)
        g_pk = lanes_of([gb_ref[0, rows, p * gp + i:p * gp + i + 1] for i in range(gp)], pw)
        hi, lo = _hi_lo(jnp.where(strict, g_pk, 0.0))
        diff = _dot(tri, hi) + _dot(tri, lo)
        decay[n, p] = jnp.where(incl, jnp.exp(jnp.where(incl, diff, 0.0)), 0.0)
    for n, p in items:
        rows = slice(n * c, (n + 1) * c)
        kcols = slice(A_KD + p * gp * A_DK, A_KD + (p + 1) * gp * A_DK)
        k_pk = qkv_ref[0, rows, kcols]
        q_pk = qkv_ref[0, rows, p * gp * A_DK:(p + 1) * gp * A_DK]
        beta_w = lanes_of([gb_ref[0, rows, LANES + p * gp + i:LANES + p * gp + i + 1] for i in range(gp)],
                          gp * A_DK)
        k32 = k_pk.astype(F32)
        kb = (k32 * beta_w).astype(BF16)
        k_rows = jnp.concatenate([jnp.where(wide_blk == i, k32, 0.0) for i in range(gp)],
                                 axis=0).astype(BF16)
        sc = _dot_nt(jnp.concatenate([kb, q_pk], axis=0), k_rows)
        a_neg[n, p] = -jnp.where(strict, sc[:c] * decay[n, p], 0.0)
        qk_scr[n, p] = jnp.where(incl, sc[c:] * decay[n, p], 0.0).astype(qk_scr.dtype)
    pw_mat = dict(a_neg)
    for key in items:
        t_mat[key] = eye + a_neg[key]
    m = 2
    while m < c:
        for key in items:
            pw_mat[key] = _dot(pw_mat[key].astype(BF16), block_diag(pw_mat[key]))
        for key in items:
            t_mat[key] = t_mat[key] + _dot(t_mat[key].astype(BF16), block_diag(pw_mat[key]))
        m *= 2
    for n, p in items:
        rows = slice(n * c, (n + 1) * c)
        xs = []
        for i in range(gp):
            h = p * gp + i
            k = qkv_ref[0, rows, A_KD + h * A_DK:A_KD + (h + 1) * A_DK].astype(F32)
            q = qkv_ref[0, rows, h * A_DK:(h + 1) * A_DK].astype(F32)
            v = qkv_ref[0, rows, 2 * A_KD + h * A_DV:2 * A_KD + (h + 1) * A_DV].astype(F32)
            gcol = gc[n][:, h:h + 1]
            bcol = gb_ref[0, rows, LANES + h:LANES + h + 1]
            egc = jnp.exp(gcol)
            kbeta = k * bcol
            xs.append(jnp.concatenate([v * bcol, kbeta * egc], axis=1).astype(BF16))
            wq_scr[n, h, c:, :] = (q * egc).astype(wq_scr.dtype)
            kd_scr[n, h] = (k * jnp.exp(gc[n][c - 1:c, h:h + 1] - gcol)).astype(kd_scr.dtype)
        sol = _dot(t_mat[n, p].astype(BF16), block_rows(xs))
        for i in range(gp):
            h = p * gp + i
            ub_scr[n, h] = sol[:, i * 2 * A_DV:i * 2 * A_DV + A_DV]
            wq_scr[n, h, :c, :] = sol[:, i * 2 * A_DV + A_DV:(i + 1) * 2 * A_DV].astype(wq_scr.dtype)

    heads = range(A_HEADS)
    for n in range(nc):
        s = [s_ref[0, h] for h in heads]
        s16 = [s[h].astype(BF16) for h in heads]
        wq = [_dot(wq_scr[n, h], s16[h]) for h in heads]
        u16 = [(ub_scr[n, h] - wq[h][:c]).astype(BF16) for h in heads]
        for p in range(npk):
            hs = [p * gp + i for i in range(gp)]
            o_pk = _dot(qk_scr[n, p], block_rows([u16[h] for h in hs]))
            o_scr[n * c:(n + 1) * c, p * gp * A_DV:(p + 1) * gp * A_DV] = (
                jnp.concatenate([wq[h][c:] for h in hs], axis=1) + o_pk)
        ku = [_dot_tn(kd_scr[n, h], u16[h]) for h in heads]
        for h in heads:
            s_ref[0, h] = s[h] * jnp.exp(gc[n][c - 1:c, h:h + 1]) + ku[h]

    _out_gate(o_scr, zg_ref, og_ref, o_ref)


def _solve_unit_lower(a, rhs, c):
    heads = range(len(a))
    p = [_hi_lo(a[h]) for h in heads]
    x = [rhs[h] - _dot_split(p[h], _hi_lo(rhs[h])) for h in heads]
    n = 2
    while n < c:
        p = [_hi_lo(_dot_split(p[h], p[h])) for h in heads]
        x = [x[h] + _dot_split(p[h], _hi_lo(x[h])) for h in heads]
        n *= 2
    return x


def _delta_small_kernel(qkv_ref, zg_ref, gb_ref, s0_ref, og_ref, o_ref, s_ref, o_scr, *, c):
    ri = lax.broadcasted_iota(jnp.int32, (c, c), 0)
    ci = lax.broadcasted_iota(jnp.int32, (c, c), 1)
    incl = ri >= ci
    strict = ri > ci
    tri = incl.astype(F32).astype(BF16)
    g0, g1, g2 = _split3(gb_ref[0, :, :LANES])
    gc = _dot(tri, g0) + (_dot(tri, g1) + _dot(tri, g2))
    gct = gc.T
    beta = gb_ref[0, :, LANES:]
    heads = range(A_HEADS)
    gcol = [gc[:, h:h + 1] for h in heads]
    glast = [gc[c - 1:c, h:h + 1] for h in heads]
    bcol = [beta[:, h:h + 1] for h in heads]
    decay = [jnp.where(incl, jnp.exp(jnp.where(incl, gcol[h] - gct[h:h + 1, :], 0.0)), 0.0) for h in heads]
    q = [qkv_ref[0, :, h * A_DK:(h + 1) * A_DK].astype(F32) for h in heads]
    k = [qkv_ref[0, :, A_KD + h * A_DK:A_KD + (h + 1) * A_DK].astype(F32) for h in heads]
    v = [qkv_ref[0, :, 2 * A_KD + h * A_DV:2 * A_KD + (h + 1) * A_DV].astype(F32) for h in heads]
    kb = [k[h] * bcol[h] for h in heads]
    egc = [jnp.exp(gcol[h]) for h in heads]
    k16 = [k[h].astype(BF16) for h in heads]
    kk = [_dot_nt(kb[h].astype(BF16), k16[h]) for h in heads]
    qk = [_dot_nt(q[h].astype(BF16), k16[h]) for h in heads]
    a_mat = [jnp.where(strict, kk[h] * decay[h], 0.0) for h in heads]
    qk = [jnp.where(incl, qk[h] * decay[h], 0.0) for h in heads]
    rhs = [jnp.concatenate([v[h] * bcol[h], kb[h] * egc[h]], axis=1) for h in heads]
    sol = _solve_unit_lower(a_mat, rhs, c)
    s = [s0_ref[0, h] for h in heads]
    s16 = [s[h].astype(BF16) for h in heads]
    ws = [_dot(sol[h][:, A_DV:].astype(BF16), s16[h]) for h in heads]
    qs = [_dot((q[h] * egc[h]).astype(BF16), s16[h]) for h in heads]
    u16 = [(sol[h][:, :A_DV] - ws[h]).astype(BF16) for h in heads]
    o = [qs[h] + _dot(qk[h].astype(BF16), u16[h]) for h in heads]
    ku = [_dot_tn((k[h] * jnp.exp(glast[h] - gcol[h])).astype(BF16), u16[h]) for h in heads]
    for h in heads:
        s_ref[0, h] = s[h] * jnp.exp(glast[h]) + ku[h]
        o_scr[:, h * A_DV:(h + 1) * A_DV] = o[h]
    _out_gate(o_scr, zg_ref, og_ref, o_ref)


def _delta(qkv, zg, gb, s0, out_g, tl, c):
    b, l, _ = qkv.shape
    tok = lambda n: pl.BlockSpec((1, tl, n), lambda i, j: (i, j, 0))
    state = pl.BlockSpec((1, A_HEADS, A_DK, A_DV), lambda i, j: (i, 0, 0, 0))
    nc = tl // c
    if c * 2 <= LANES and LANES % c == 0 and c % 16 == 0:
        gp = LANES // c
        kern = functools.partial(_delta_packed_kernel, tl=tl, c=c)
        scratch = [pltpu.VMEM((nc, A_HEADS, 2 * c, A_DK), BF16),
                   pltpu.VMEM((nc, A_HEADS, c, A_DV), F32),
                   pltpu.VMEM((nc, A_HEADS, c, A_DK), BF16),
                   pltpu.VMEM((nc, A_HEADS // gp, c, LANES), BF16),
                   pltpu.VMEM((tl, A_VD), F32)]
    else:
        assert tl == c and l == tl
        kern = functools.partial(_delta_small_kernel, c=c)
        scratch = [pltpu.VMEM((tl, A_VD), F32)]
    return pl.pallas_call(
        kern,
        grid=(b, l // tl),
        in_specs=[tok(A_CONV_CH), tok(A_VD), tok(2 * LANES), state, _const_spec(out_g.shape)],
        out_specs=[tok(A_VD), state],
        out_shape=[jax.ShapeDtypeStruct((b, l, A_VD), BF16),
                   jax.ShapeDtypeStruct((b, A_HEADS, A_DK, A_DV), F32)],
        scratch_shapes=scratch,
        compiler_params=_params(2),
        name="delta",
    )(qkv, zg, gb, s0, out_g)


def _post_body(mix_ref, res_ref, pe_ref, wmix_ref, fg_ref, wg_ref, wu_ref, wd_ref,
               pg_ref, wpg_ref, wpp_ref, n_chunks):
    h = res_ref[...] + _dot(mix_ref[...], wmix_ref[...])
    xn = _rms(h, fg_ref[...]).astype(BF16)
    fc = FFN_HIDDEN // n_chunks
    acc = h
    for i in range(n_chunks):
        cs = slice(i * fc, (i + 1) * fc)
        gt = _dot(xn, wg_ref[:, cs])
        up = _dot(xn, wu_ref[:, cs])
        acc = acc + _dot((_silu(gt) * up).astype(BF16), wd_ref[cs, :])
    h2 = acc
    gate = _sigmoid(_dot(_rms(h2, pg_ref[...]).astype(BF16), wpg_ref[...]))
    return h2 + _dot(pe_ref[...].astype(BF16), wpp_ref[...]) * gate


def _post_mid_kernel(mix_ref, res_ref, pe_ref, wmix_ref, fg_ref, wg_ref, wu_ref, wd_ref,
                     pg_ref, wpg_ref, wpp_ref, kvn_ref, kvw_ref, bn_ref, wq_ref,
                     h_ref, q_ref, kv_ref, *, n_chunks):
    h3 = _post_body(mix_ref, res_ref, pe_ref, wmix_ref, fg_ref, wg_ref, wu_ref, wd_ref,
                    pg_ref, wpg_ref, wpp_ref, n_chunks)
    h_ref[...] = h3
    kv_ref[...] = _dot(_rms(h3, kvn_ref[...]).astype(BF16), kvw_ref[...])
    q = _dot(_rms(h3, bn_ref[...]).astype(BF16), wq_ref[...]) * (B_HEAD_DIM ** -0.5)
    q_ref[...] = q.astype(q_ref.dtype)


def _post_last_kernel(mix_ref, res_ref, pe_ref, wmix_ref, fg_ref, wg_ref, wu_ref, wd_ref,
                      pg_ref, wpg_ref, wpp_ref, fn_ref, y_ref, *, n_chunks):
    h3 = _post_body(mix_ref, res_ref, pe_ref, wmix_ref, fg_ref, wg_ref, wu_ref, wd_ref,
                    pg_ref, wpg_ref, wpp_ref, n_chunks)
    y_ref[...] = _rms(h3, fn_ref[...])


def _post(mix, res, pe, common, tail, tm, last, q_dtype=BF16):
    m = res.shape[0]
    row = lambda n: pl.BlockSpec((tm, n), lambda i: (i, 0))
    weights = list(common) + list(tail)
    in_specs = [row(mix.shape[1]), row(D_MODEL), row(PLE_DIM)] + [_const_spec(w.shape) for w in weights]
    if last:
        kern = _post_last_kernel
        out_specs = [row(D_MODEL)]
        out_shape = [jax.ShapeDtypeStruct((m, D_MODEL), F32)]
    else:
        kern = _post_mid_kernel
        out_specs = [row(D_MODEL), row(B_QD), row(2 * B_KVD)]
        out_shape = [jax.ShapeDtypeStruct((m, D_MODEL), F32),
                     jax.ShapeDtypeStruct((m, B_QD), q_dtype),
                     jax.ShapeDtypeStruct((m, 2 * B_KVD), F32)]
    return pl.pallas_call(
        functools.partial(kern, n_chunks=2),
        grid=(m // tm,),
        in_specs=in_specs,
        out_specs=out_specs,
        out_shape=out_shape,
        compiler_params=_params(1),
        name="post_last" if last else "post_mid",
    )(mix, res, pe, *weights)


def _attn_prompt_kernel(sink_ref, q_ref, kvp_ref, kvc_ref, o_ref):
    i = pl.program_id(1)
    kv = jnp.concatenate([kvp_ref[0], kvc_ref[0]], axis=0)
    k16 = kv[:, :B_KVD].astype(BF16)
    v16 = kv[:, B_KVD:].astype(BF16)
    qi = lax.broadcasted_iota(jnp.int32, (WINDOW, 2 * WINDOW), 0)
    ki = lax.broadcasted_iota(jnp.int32, (WINDOW, 2 * WINDOW), 1)
    rel = qi + WINDOW - ki
    kpos = (i - 1) * WINDOW + ki
    mask = (rel >= 0) & (rel < WINDOW) & (kpos >= 0)
    outs = []
    for j in range(B_KV_HEADS):
        kj = k16[:, j * B_HEAD_DIM:(j + 1) * B_HEAD_DIM]
        vj = v16[:, j * B_HEAD_DIM:(j + 1) * B_HEAD_DIM]
        for g in range(B_GROUP):
            hq = j * B_GROUP + g
            qh = q_ref[0, :, hq * B_HEAD_DIM:(hq + 1) * B_HEAD_DIM]
            s = jnp.where(mask, _dot_nt(qh, kj), NEG_INF)
            sk = sink_ref[hq]
            mx = jnp.maximum(jnp.max(s, axis=-1, keepdims=True), sk)
            p = jnp.exp(s - mx)
            den = jnp.sum(p, axis=-1, keepdims=True) + jnp.exp(sk - mx)
            outs.append(_dot(p.astype(BF16), vj) / den)
    o_ref[0] = jnp.concatenate(outs, axis=1).astype(o_ref.dtype)


def _attn_prompt(q, kv, sinks):
    b, l, _ = q.shape
    nb = l // WINDOW
    return pl.pallas_call(
        _attn_prompt_kernel,
        grid=(b, nb),
        in_specs=[pl.BlockSpec(memory_space=pltpu.SMEM),
                  pl.BlockSpec((1, WINDOW, B_QD), lambda i, j: (i, j, 0)),
                  pl.BlockSpec((1, WINDOW, 2 * B_KVD), lambda i, j: (i, jnp.maximum(j - 1, 0), 0)),
                  pl.BlockSpec((1, WINDOW, 2 * B_KVD), lambda i, j: (i, j, 0))],
        out_specs=pl.BlockSpec((1, WINDOW, B_QD), lambda i, j: (i, j, 0)),
        out_shape=jax.ShapeDtypeStruct((b, l, B_QD), BF16),
        compiler_params=_params(2),
        name="attn_prompt",
    )(sinks, q, kv, kv)


def _attn_sample_kernel(sink_ref, q_ref, kv_ref, ck_ref, cv_ref, o_ref, *, bb, t, nbuf):
    rows = B_GROUP * t
    qi = lax.broadcasted_iota(jnp.int32, (rows, nbuf + t), 0) % t
    ki = lax.broadcasted_iota(jnp.int32, (rows, nbuf + t), 1)
    rel = qi + nbuf - ki
    mask = (rel >= 0) & (rel < WINDOW)
    grp = lax.broadcasted_iota(jnp.int32, (rows, 1), 0) // t
    items = [(b, j) for b in range(bb) for j in range(B_KV_HEADS)]
    sink_col = {}
    for j in range(B_KV_HEADS):
        col = jnp.full((rows, 1), sink_ref[j * B_GROUP + B_GROUP - 1], F32)
        for g in range(B_GROUP - 2, -1, -1):
            col = jnp.where(grp == g, sink_ref[j * B_GROUP + g], col)
        sink_col[j] = col
    keys, vals = {}, {}
    for b in range(bb):
        kv = kv_ref[b]
        keys[b] = jnp.concatenate([ck_ref[b], kv[:, :B_KVD]], axis=0).astype(BF16)
        vals[b] = jnp.concatenate([cv_ref[b], kv[:, B_KVD:]], axis=0).astype(BF16)
    s = {}
    for b, j in items:
        qs = jnp.concatenate(
            [q_ref[b, :, (j * B_GROUP + g) * B_HEAD_DIM:(j * B_GROUP + g + 1) * B_HEAD_DIM]
             for g in range(B_GROUP)], axis=0).astype(BF16)
        s[b, j] = jnp.where(mask, _dot_nt(qs, keys[b][:, j * B_HEAD_DIM:(j + 1) * B_HEAD_DIM]), NEG_INF)
    p, den = {}, {}
    for b, j in items:
        mx = jnp.maximum(jnp.max(s[b, j], axis=-1, keepdims=True), sink_col[j])
        p[b, j] = jnp.exp(s[b, j] - mx)
        den[b, j] = jnp.sum(p[b, j], axis=-1, keepdims=True) + jnp.exp(sink_col[j] - mx)
    o = {}
    for b, j in items:
        o[b, j] = _dot(p[b, j].astype(BF16), vals[b][:, j * B_HEAD_DIM:(j + 1) * B_HEAD_DIM]) / den[b, j]
    for b in range(bb):
        o_ref[b] = jnp.concatenate(
            [o[b, j][g * t:(g + 1) * t, :] for j in range(B_KV_HEADS) for g in range(B_GROUP)],
            axis=1).astype(o_ref.dtype)


def _attn_sample(q, kv, ck, cv, sinks, bb):
    b, t, _ = q.shape
    nbuf = ck.shape[1]
    per_b = lambda r, n: pl.BlockSpec((bb, r, n), lambda i: (i, 0, 0))
    return pl.pallas_call(
        functools.partial(_attn_sample_kernel, bb=bb, t=t, nbuf=nbuf),
        grid=(b // bb,),
        in_specs=[pl.BlockSpec(memory_space=pltpu.SMEM),
                  per_b(t, B_QD), per_b(t, 2 * B_KVD), per_b(nbuf, B_KVD), per_b(nbuf, B_KVD)],
        out_specs=per_b(t, B_QD),
        out_shape=jax.ShapeDtypeStruct((b, t, B_QD), BF16),
        compiler_params=_params(1),
        name="attn_sample",
    )(sinks, q, kv, ck, cv)


def _pad_lanes(v):
    return jnp.pad(v, ((0, 0), (0, LANES - v.shape[1])))


def _prepare(p):
    w_in = p["a_w_in"][0]
    n_gate = A_CONV_CH + A_VD
    row = lambda v: v.reshape(1, -1).astype(F32)

    def post_common(i, w_mix):
        gu = p["ffn_w_gu"][i]
        return (w_mix.astype(BF16), row(p["ffn_norm"][i]),
                gu[:, :FFN_HIDDEN].astype(BF16), gu[:, FFN_HIDDEN:].astype(BF16),
                p["ffn_w_down"][i].astype(BF16), row(p["ple_norm"][i]),
                p["ple_w_gate"][i].astype(BF16), p["ple_w_proj"][i].astype(BF16))

    return {
        "a_norm": row(p["a_norm"][0]),
        "w_qkv": w_in[:, :A_CONV_CH].astype(BF16),
        "w_z": w_in[:, A_CONV_CH:n_gate].astype(BF16),
        "w_ab": jnp.concatenate([_pad_lanes(w_in[:, n_gate:n_gate + A_HEADS]),
                                 _pad_lanes(w_in[:, n_gate + A_HEADS:])], axis=1).astype(BF16),
        "conv_w": p["a_conv_w"][0],
        "gate_prm": jnp.concatenate([_pad_lanes(row(p["a_a_log"][0])),
                                     _pad_lanes(row(p["a_dt_bias"][0]))], axis=0),
        "out_g": row(p["a_out_norm"][0]),
        "post0": post_common(0, p["a_w_out"][0]),
        "tail0": (row(p["kv_norm"]), p["kv_w"].astype(BF16), row(p["b_norm"][0]), p["b_w_q"][0].astype(BF16)),
        "post1": post_common(1, p["b_w_o"][0]),
        "tail1": (row(p["final_norm"]),),
        "sinks": p["b_sinks"][0].astype(F32),
    }


def _tiles(b, l, prompt):
    m = b * l
    c = min(CHUNK, l)
    if prompt:
        return dict(tm=min(512, m), pc_bb=1, pc_tl=min(512, l), dl_tl=min(256, l), c=c, at_bb=1)
    return dict(tm=min(512, m), pc_bb=min(32, b), pc_tl=l, dl_tl=l, c=c, at_bb=min(8, b))


def _trunk(x, pe, s_init, c_init, k_buf, v_buf, w):
    b, l, _ = x.shape
    m = b * l
    prompt = k_buf is None
    ts = _tiles(b, l, prompt)
    act_dtype = BF16 if prompt else F32
    qkv, zg, gb, c_new = _proj_conv(x, c_init, w, ts["pc_bb"], ts["pc_tl"], act_dtype)
    o, s_new = _delta(qkv, zg, gb, s_init, w["out_g"], ts["dl_tl"], ts["c"])
    x2d = x.reshape(m, D_MODEL)
    h, q, kv = _post(o.reshape(m, -1), x2d, pe[0].reshape(m, PLE_DIM), w["post0"], w["tail0"],
                     ts["tm"], False, act_dtype)
    q3 = q.reshape(b, l, -1)
    kv3 = kv.reshape(b, l, -1)
    k_sh = kv3[:, :, :B_KVD].reshape(b, l, B_KV_HEADS, B_HEAD_DIM)
    v_sh = kv3[:, :, B_KVD:].reshape(b, l, B_KV_HEADS, B_HEAD_DIM)
    if prompt:
        att = _attn_prompt(q3, kv3, w["sinks"])
        k_win = k_sh[:, -WINDOW:]
        v_win = v_sh[:, -WINDOW:]
    else:
        nbuf = k_buf.shape[1]
        att = _attn_sample(q3, kv3, k_buf.reshape(b, nbuf, B_KVD), v_buf.reshape(b, nbuf, B_KVD),
                           w["sinks"], ts["at_bb"])
        k_win = jnp.concatenate([k_buf, k_sh], axis=1)[:, -nbuf:]
        v_win = jnp.concatenate([v_buf, v_sh], axis=1)[:, -nbuf:]
    (y,) = _post(att.reshape(m, -1), h, pe[1].reshape(m, PLE_DIM), w["post1"], w["tail1"], ts["tm"], True)
    return y.reshape(b, l, D_MODEL), s_new[None], c_new[None], k_win, v_win


def kernel(x_prompt, x_sample, state_delta, state_conv, cache_k_win, cache_v_win, p_prompt, p_sample, a_norm, a_w_in, a_conv_w, a_a_log, a_dt_bias, a_out_norm, a_w_out, kv_norm, kv_w, b_norm, b_w_q, b_sinks, b_w_o, ffn_norm, ffn_w_gu, ffn_w_down, ple_norm, ple_w_proj, ple_w_gate, final_norm):
    w = _prepare({
        "a_norm": a_norm, "a_w_in": a_w_in, "a_conv_w": a_conv_w, "a_a_log": a_a_log,
        "a_dt_bias": a_dt_bias, "a_out_norm": a_out_norm, "a_w_out": a_w_out,
        "kv_norm": kv_norm, "kv_w": kv_w, "b_norm": b_norm, "b_w_q": b_w_q, "b_sinks": b_sinks,
        "b_w_o": b_w_o, "ffn_norm": ffn_norm, "ffn_w_gu": ffn_w_gu, "ffn_w_down": ffn_w_down,
        "ple_norm": ple_norm, "ple_w_proj": ple_w_proj, "ple_w_gate": ple_w_gate,
        "final_norm": final_norm,
    })
    bp = x_prompt.shape[0]
    s0 = jnp.zeros((bp, A_HEADS, A_DK, A_DV), F32)
    c0 = jnp.zeros((bp, CONV_W - 1, A_CONV_CH), x_prompt.dtype)
    y_p, sd_p, sc_p, kw_p, vw_p = _trunk(x_prompt, p_prompt, s0, c0, None, None, w)
    y_s, sd_s, sc_s, kw_s, vw_s = _trunk(x_sample, p_sample, state_delta[0], state_conv[0],
                                         cache_k_win, cache_v_win, w)
    return (y_p, y_s, sd_p, sd_s, sc_p, sc_s, kw_p, kw_s, vw_p, vw_s)
```

```python
import functools

import jax
import jax.numpy as jnp
from jax import lax
from jax.experimental import pallas as pl
from jax.experimental.pallas import tpu as pltpu

F32 = jnp.float32
BF16 = jnp.bfloat16

D_MODEL = 1024
A_HEADS = 8
A_DK = 128
A_DV = 128
A_KD = A_HEADS * A_DK
A_VD = A_HEADS * A_DV
A_CONV_CH = 2 * A_KD + A_VD
CONV_W = 4
CHUNK = 64
B_Q_HEADS = 16
B_KV_HEADS = 4
B_HEAD_DIM = 64
B_GROUP = B_Q_HEADS // B_KV_HEADS
B_QD = B_Q_HEADS * B_HEAD_DIM
B_KVD = B_KV_HEADS * B_HEAD_DIM
WINDOW = 128
FFN_HIDDEN = 2816
PLE_DIM = 256
EPS = 1e-6
L2_EPS = 1e-6
NEG_INF = -1e30

LANES = 128
SUBLANES = 8
VMEM_LIMIT_BYTES = 56 * 1024 * 1024
PROJ_SLAB = 512


def _rms(x, g):
    return x * lax.rsqrt(jnp.mean(x * x, axis=-1, keepdims=True) + EPS) * g


def _sigmoid(x):
    return 1.0 / (1.0 + jnp.exp(-x))


def _silu(x):
    return x * _sigmoid(x)


def _softplus(x):
    return jnp.maximum(x, 0.0) + jnp.log1p(jnp.exp(-jnp.abs(x)))


def _dot(a, b):
    return jnp.dot(a, b, preferred_element_type=F32)


def _dot_nt(a, b):
    return lax.dot_general(a, b, (((1,), (1,)), ((), ())), preferred_element_type=F32)


def _dot_tn(a, b):
    return lax.dot_general(a, b, (((0,), (0,)), ((), ())), preferred_element_type=F32)


def _hi_lo(x):
    hi = x.astype(BF16)
    return hi, (x - hi.astype(F32)).astype(BF16)


def _split3(x):
    p0 = x.astype(BF16)
    r = x - p0.astype(F32)
    p1 = r.astype(BF16)
    p2 = (r - p1.astype(F32)).astype(BF16)
    return p0, p1, p2


def _const_spec(shape):
    n = len(shape)
    return pl.BlockSpec(shape, lambda *_: (0,) * n, pipeline_mode=pl.Buffered(1))


def _params(n_axes):
    return pltpu.CompilerParams(dimension_semantics=("arbitrary",) * n_axes,
                                vmem_limit_bytes=VMEM_LIMIT_BYTES)


def _proj_conv_kernel(x_ref, cb_ref, ng_ref, wqkv_ref, wz_ref, wab_ref, cw_ref, gp_ref,
                      qkv_ref, zg_ref, gb_ref, c_ref, ext_scr, *, bb, tl):
    hist = CONV_W - 1
    pad = SUBLANES
    m = bb * tl

    @pl.when(pl.program_id(1) == 0)
    def _():
        ext_scr[:, pad - hist:pad, :] = cb_ref[...]

    xn = _rms(x_ref[...].reshape(m, D_MODEL), ng_ref[...]).astype(BF16)

    for s in range(A_CONV_CH // PROJ_SLAB):
        c0 = s * PROJ_SLAB
        cols = slice(c0, c0 + PROJ_SLAB)
        p = _dot(xn, wqkv_ref[:, cols]).reshape(bb, tl, PROJ_SLAB)
        ext_scr[:, pad:pad + tl, cols] = p
        conv = p * cw_ref[hist:hist + 1, cols]
        for j in range(1, CONV_W):
            conv = conv + ext_scr[:, pad - j:pad - j + tl, cols] * cw_ref[hist - j:hist - j + 1, cols]
        tail = p[:, tl - hist:, :]
        ext_scr[:, pad - hist:pad, cols] = tail
        c_ref[:, :, cols] = tail
        act = _silu(conv)
        if c0 < 2 * A_KD:
            scale = A_DK ** -0.5 if c0 < A_KD else 1.0
            for h in range(PROJ_SLAB // A_DK):
                a = act[:, :, h * A_DK:(h + 1) * A_DK]
                a = a * lax.rsqrt(jnp.sum(a * a, axis=-1, keepdims=True) + L2_EPS) * scale
                qkv_ref[:, :, c0 + h * A_DK:c0 + (h + 1) * A_DK] = a.astype(qkv_ref.dtype)
        else:
            qkv_ref[:, :, cols] = act.astype(qkv_ref.dtype)

    for s in range(A_VD // PROJ_SLAB):
        cols = slice(s * PROJ_SLAB, (s + 1) * PROJ_SLAB)
        zg = _silu(_dot(xn, wz_ref[:, cols]))
        zg_ref[:, :, cols] = zg.reshape(bb, tl, PROJ_SLAB).astype(zg_ref.dtype)

    ab = _dot(xn, wab_ref[...])
    g = -jnp.exp(gp_ref[0:1, :]) * _softplus(ab[:, :LANES] + gp_ref[1:2, :])
    beta = _sigmoid(ab[:, LANES:])
    gb_ref[...] = jnp.concatenate([g, beta], axis=1).reshape(bb, tl, 2 * LANES)


def _proj_conv(x, cbuf, w, bb, tl, act_dtype):
    b, l, _ = x.shape
    tok = lambda n: pl.BlockSpec((bb, tl, n), lambda i, j: (i, j, 0))
    hist = pl.BlockSpec((bb, CONV_W - 1, A_CONV_CH), lambda i, j: (i, 0, 0))
    consts = [w["a_norm"], w["w_qkv"], w["w_z"], w["w_ab"], w["conv_w"], w["gate_prm"]]
    return pl.pallas_call(
        functools.partial(_proj_conv_kernel, bb=bb, tl=tl),
        grid=(b // bb, l // tl),
        in_specs=[tok(D_MODEL), hist] + [_const_spec(c.shape) for c in consts],
        out_specs=[tok(A_CONV_CH), tok(A_VD), tok(2 * LANES), hist],
        out_shape=[jax.ShapeDtypeStruct((b, l, A_CONV_CH), act_dtype),
                   jax.ShapeDtypeStruct((b, l, A_VD), act_dtype),
                   jax.ShapeDtypeStruct((b, l, 2 * LANES), F32),
                   jax.ShapeDtypeStruct((b, CONV_W - 1, A_CONV_CH), F32)],
        scratch_shapes=[pltpu.VMEM((bb, tl + SUBLANES, A_CONV_CH), F32)],
        compiler_params=_params(2),
        name="proj_conv",
    )(x, cbuf, *consts)


def _out_gate(o_scr, zg_ref, og_ref, o_ref):
    for h in range(A_HEADS):
        vs = slice(h * A_DV, (h + 1) * A_DV)
        oh = o_scr[:, vs]
        oh = oh * lax.rsqrt(jnp.mean(oh * oh, axis=-1, keepdims=True) + EPS) * og_ref[...]
        o_ref[0, :, vs] = (oh * zg_ref[0, :, vs].astype(F32)).astype(o_ref.dtype)


def _delta_packed_kernel(qkv_ref, zg_ref, gb_ref, s0_ref, og_ref, o_ref, s_ref,
                         wq_scr, ub_scr, kd_scr, qk_scr, o_scr, *, tl, c, carry):
    gp = min(LANES // c, A_HEADS)
    npk = A_HEADS // gp
    pw = gp * c
    nc = tl // c
    st = (lambda n, h: (0, h)) if carry else (lambda n, h: (n, h))

    if carry:
        @pl.when(pl.program_id(1) == 0)
        def _():
            s_ref[...] = s0_ref[...]

    ri = lax.broadcasted_iota(jnp.int32, (c, pw), 0)
    li = lax.broadcasted_iota(jnp.int32, (c, pw), 1)
    cj = li % c
    blk = li // c
    incl = ri >= cj
    strict = ri > cj
    eye = (ri == cj).astype(F32)
    tri = (lax.broadcasted_iota(jnp.int32, (c, c), 0) >= lax.broadcasted_iota(jnp.int32, (c, c), 1))
    tri = tri.astype(F32).astype(BF16)
    wide_blk = lax.broadcasted_iota(jnp.int32, (c, gp * A_DK), 1) // A_DK

    def lanes_of(cols, width):
        sel = blk if width == pw else wide_blk
        out = jnp.broadcast_to(cols[-1], (c, width))
        for i in range(gp - 2, -1, -1):
            out = jnp.where(sel == i, jnp.broadcast_to(cols[i], (c, width)), out)
        return out

    def block_diag(y):
        return jnp.concatenate([jnp.where(blk == i, y, 0.0) for i in range(gp)], axis=0).astype(BF16)

    def block_rows(parts):
        w = parts[0].shape[1]
        wide = jnp.concatenate(parts, axis=1)
        sel = lax.broadcasted_iota(jnp.int32, (c, gp * w), 1) // w
        return jnp.concatenate([jnp.where(sel == i, wide, 0.0) for i in range(gp)], axis=0).astype(BF16)

    items = [(n, p) for n in range(nc) for p in range(npk)]
    gc = []
    for n in range(nc):
        g0, g1, g2 = _split3(gb_ref[0, n * c:(n + 1) * c, :LANES])
        gc.append(_dot(tri, g0) + (_dot(tri, g1) + _dot(tri, g2)))
    decay, a_neg, t_mat = {}, {}, {}
    for n, p in items:
        rows = slice(n * c, (n + 1) * c)
        g_pk = lanes_of([gb_ref[0, rows, p * gp + i:p * gp + i + 1] for i in range(gp)], pw)
        hi, lo = _hi_lo(jnp.where(strict, g_pk, 0.0))
        diff = _dot(tri, hi) + _dot(tri, lo)
        decay[n, p] = jnp.where(incl, jnp.exp(jnp.where(incl, diff, 0.0)), 0.0)
    for n, p in items:
        rows = slice(n * c, (n + 1) * c)
        kcols = slice(A_KD + p * gp * A_DK, A_KD + (p + 1) * gp * A_DK)
        k_pk = qkv_ref[0, rows, kcols]
        q_pk = qkv_ref[0, rows, p * gp * A_DK:(p + 1) * gp * A_DK]
        beta_w = lanes_of([gb_ref[0, rows, LANES + p * gp + i:LANES + p * gp + i + 1] for i in range(gp)],
                          gp * A_DK)
        k32 = k_pk.astype(F32)
        k_rows = jnp.concatenate([jnp.where(wide_blk == i, k32, 0.0) for i in range(gp)],
                                 axis=0).astype(BF16)
        kk = _dot_nt((k32 * beta_w).astype(BF16), k_rows)
        qk = _dot_nt(q_pk.astype(BF16), k_rows)
        a_neg[n, p] = -jnp.where(strict, kk * decay[n, p], 0.0)
        qk_scr[n, p] = jnp.where(incl, qk * decay[n, p], 0.0).astype(qk_scr.dtype)
    pw_mat = dict(a_neg)
    for key in items:
        t_mat[key] = eye + a_neg[key]
    m = 2
    while m < c:
        for key in items:
            pw_mat[key] = _dot(pw_mat[key].astype(BF16), block_diag(pw_mat[key]))
        for key in items:
            t_mat[key] = t_mat[key] + _dot(t_mat[key].astype(BF16), block_diag(pw_mat[key]))
        m *= 2
    for n, p in items:
        rows = slice(n * c, (n + 1) * c)
        xs = []
        for i in range(gp):
            h = p * gp + i
            k = qkv_ref[0, rows, A_KD + h * A_DK:A_KD + (h + 1) * A_DK].astype(F32)
            q = qkv_ref[0, rows, h * A_DK:(h + 1) * A_DK].astype(F32)
            v = qkv_ref[0, rows, 2 * A_KD + h * A_DV:2 * A_KD + (h + 1) * A_DV].astype(F32)
            gcol = gc[n][:, h:h + 1]
            bcol = gb_ref[0, rows, LANES + h:LANES + h + 1]
            egc = jnp.exp(gcol)
            kbeta = k * bcol
            xs.append(jnp.concatenate([v * bcol, kbeta * egc], axis=1))
            wq_scr[n, h, c:, :] = (q * egc).astype(wq_scr.dtype)
            kd_scr[n, h] = (k * jnp.exp(gc[n][c - 1:c, h:h + 1] - gcol)).astype(kd_scr.dtype)
        sol = _dot(t_mat[n, p].astype(BF16), block_rows(xs))
        for i in range(gp):
            h = p * gp + i
            ub_scr[n, h] = sol[:, i * 2 * A_DV:i * 2 * A_DV + A_DV]
            wq_scr[n, h, :c, :] = sol[:, i * 2 * A_DV + A_DV:(i + 1) * 2 * A_DV].astype(wq_scr.dtype)

    heads = range(A_HEADS)
    for n in range(nc):
        s = [(s_ref if carry else s0_ref)[st(n, h)] for h in heads]
        s16 = [s[h].astype(BF16) for h in heads]
        wq = [_dot(wq_scr[n, h].astype(BF16), s16[h]) for h in heads]
        u = [ub_scr[n, h] - wq[h][:c] for h in heads]
        for p in range(npk):
            hs = [p * gp + i for i in range(gp)]
            o_pk = _dot(qk_scr[n, p].astype(BF16), block_rows([u[h] for h in hs]))
            o_scr[n * c:(n + 1) * c, p * gp * A_DV:(p + 1) * gp * A_DV] = (
                jnp.concatenate([wq[h][c:] for h in hs], axis=1) + o_pk)
        ku = [_dot_tn(kd_scr[n, h].astype(BF16), u[h].astype(BF16)) for h in heads]
        for h in heads:
            s_ref[st(n, h)] = s[h] * jnp.exp(gc[n][c - 1:c, h:h + 1]) + ku[h]

    _out_gate(o_scr, zg_ref, og_ref, o_ref)


def _delta(qkv, zg, gb, s0, out_g, tl, c, carry):
    b, l, _ = qkv.shape
    nc = tl // c
    gp = min(LANES // c, A_HEADS)
    assert carry or l == tl
    tok = lambda n: pl.BlockSpec((1, tl, n), lambda i, j: (i, j, 0))
    state = pl.BlockSpec((1 if carry else nc, A_HEADS, A_DK, A_DV), lambda i, j: (i, 0, 0, 0))
    mm_dtype = BF16 if c % 16 == 0 else F32
    return pl.pallas_call(
        functools.partial(_delta_packed_kernel, tl=tl, c=c, carry=carry),
        grid=(b, l // tl),
        in_specs=[tok(A_CONV_CH), tok(A_VD), tok(2 * LANES), state, _const_spec(out_g.shape)],
        out_specs=[tok(A_VD), state],
        out_shape=[jax.ShapeDtypeStruct((b, l, A_VD), BF16), jax.ShapeDtypeStruct(s0.shape, F32)],
        scratch_shapes=[pltpu.VMEM((nc, A_HEADS, 2 * c, A_DK), mm_dtype),
                        pltpu.VMEM((nc, A_HEADS, c, A_DV), F32),
                        pltpu.VMEM((nc, A_HEADS, c, A_DK), mm_dtype),
                        pltpu.VMEM((nc, A_HEADS // gp, c, gp * c), mm_dtype),
                        pltpu.VMEM((tl, A_VD), F32)],
        compiler_params=_params(2),
        name="delta",
    )(qkv, zg, gb, s0, out_g)


def _post_body(mix_ref, res_ref, pe_ref, wmix_ref, fg_ref, wgu_ref, wd_ref,
               pg_ref, wpg_ref, wpp_ref, n_chunks):
    h = res_ref[...] + _dot(mix_ref[...], wmix_ref[...])
    xn = _rms(h, fg_ref[...]).astype(BF16)
    fc = FFN_HIDDEN // n_chunks
    acc = h
    for i in range(n_chunks):
        gt = _dot(xn, wgu_ref[:, i * fc:(i + 1) * fc])
        up = _dot(xn, wgu_ref[:, FFN_HIDDEN + i * fc:FFN_HIDDEN + (i + 1) * fc])
        acc = acc + _dot((_silu(gt) * up).astype(BF16), wd_ref[i * fc:(i + 1) * fc, :])
    h2 = acc
    gate = _sigmoid(_dot(_rms(h2, pg_ref[...]).astype(BF16), wpg_ref[...]))
    return h2 + _dot(pe_ref[...].astype(BF16), wpp_ref[...]) * gate


def _post_mid_kernel(mix_ref, res_ref, pe_ref, wmix_ref, fg_ref, wgu_ref, wd_ref,
                     pg_ref, wpg_ref, wpp_ref, kvn_ref, kvw_ref, bn_ref, wq_ref,
                     h_ref, q_ref, kv_ref, *, n_chunks):
    h3 = _post_body(mix_ref, res_ref, pe_ref, wmix_ref, fg_ref, wgu_ref, wd_ref,
                    pg_ref, wpg_ref, wpp_ref, n_chunks)
    h_ref[...] = h3
    kv_ref[...] = _dot(_rms(h3, kvn_ref[...]).astype(BF16), kvw_ref[...])
    q = _dot(_rms(h3, bn_ref[...]).astype(BF16), wq_ref[...]) * (B_HEAD_DIM ** -0.5)
    q_ref[...] = q.astype(q_ref.dtype)


def _post_last_kernel(mix_ref, res_ref, pe_ref, wmix_ref, fg_ref, wgu_ref, wd_ref,
                      pg_ref, wpg_ref, wpp_ref, fn_ref, y_ref, *, n_chunks):
    h3 = _post_body(mix_ref, res_ref, pe_ref, wmix_ref, fg_ref, wgu_ref, wd_ref,
                    pg_ref, wpg_ref, wpp_ref, n_chunks)
    y_ref[...] = _rms(h3, fn_ref[...])


def _post(mix, res, pe, layer, common, tail, tm, last, q_dtype=BF16):
    m = res.shape[0]
    row = lambda n: pl.BlockSpec((tm, n), lambda i: (i, 0))
    weights = list(common) + list(tail)
    pe_spec = pl.BlockSpec((None, tm, PLE_DIM), lambda i: (layer, i, 0))
    in_specs = [row(mix.shape[1]), row(D_MODEL), pe_spec] + [_const_spec(w.shape) for w in weights]
    if last:
        kern = _post_last_kernel
        out_specs = [row(D_MODEL)]
        out_shape = [jax.ShapeDtypeStruct((m, D_MODEL), F32)]
    else:
        kern = _post_mid_kernel
        out_specs = [row(D_MODEL), row(B_QD), row(2 * B_KVD)]
        out_shape = [jax.ShapeDtypeStruct((m, D_MODEL), F32),
                     jax.ShapeDtypeStruct((m, B_QD), q_dtype),
                     jax.ShapeDtypeStruct((m, 2 * B_KVD), F32)]
    return pl.pallas_call(
        functools.partial(kern, n_chunks=2),
        grid=(m // tm,),
        in_specs=in_specs,
        out_specs=out_specs,
        out_shape=out_shape,
        compiler_params=_params(1),
        name="post_last" if last else "post_mid",
    )(mix, res, pe, *weights)


def _attn_prompt_kernel(sink_ref, q_ref, kvp_ref, kvc_ref, o_ref):
    i = pl.program_id(1)
    rows = B_GROUP * WINDOW
    k2 = jnp.concatenate([kvc_ref[0, :, :B_KVD], kvp_ref[0, :, :B_KVD]], axis=0).astype(BF16)
    v2 = jnp.concatenate([kvc_ref[0, :, B_KVD:], kvp_ref[0, :, B_KVD:]], axis=0).astype(BF16)
    qi = lax.broadcasted_iota(jnp.int32, (rows, WINDOW), 0) % WINDOW
    ci = lax.broadcasted_iota(jnp.int32, (rows, WINDOW), 1)
    own = ci <= qi
    prev_ok = (ci > qi) & ((i - 1) * WINDOW + ci >= 0)
    grp = lax.broadcasted_iota(jnp.int32, (rows, 1), 0) // WINDOW
    kv_heads = range(B_KV_HEADS)
    sc = []
    for j in kv_heads:
        qs = jnp.concatenate(
            [q_ref[0, :, (j * B_GROUP + g) * B_HEAD_DIM:(j * B_GROUP + g + 1) * B_HEAD_DIM]
             for g in range(B_GROUP)], axis=0)
        sc.append(_dot_nt(qs, k2[:, j * B_HEAD_DIM:(j + 1) * B_HEAD_DIM]))
    pv = []
    for j in kv_heads:
        sink = jnp.full((rows, 1), sink_ref[j * B_GROUP + B_GROUP - 1], F32)
        for g in range(B_GROUP - 2, -1, -1):
            sink = jnp.where(grp == g, sink_ref[j * B_GROUP + g], sink)
        s = jnp.where(own, sc[j][:, :WINDOW], jnp.where(prev_ok, sc[j][:, WINDOW:], NEG_INF))
        mx = jnp.maximum(jnp.max(s, axis=-1, keepdims=True), sink)
        p = jnp.exp(s - mx)
        den = jnp.sum(p, axis=-1, keepdims=True) + jnp.exp(sink - mx)
        p2 = jnp.concatenate([jnp.where(own, p, 0.0), jnp.where(own, 0.0, p)], axis=1).astype(BF16)
        pv.append((p2, den))
    outs = []
    for j in kv_heads:
        o = _dot(pv[j][0], v2[:, j * B_HEAD_DIM:(j + 1) * B_HEAD_DIM]) / pv[j][1]
        outs += [o[g * WINDOW:(g + 1) * WINDOW, :] for g in range(B_GROUP)]
    o_ref[0] = jnp.concatenate(outs, axis=1).astype(o_ref.dtype)


def _attn_prompt(q, kv, sinks):
    b, l, _ = q.shape
    nb = l // WINDOW
    return pl.pallas_call(
        _attn_prompt_kernel,
        grid=(b, nb),
        in_specs=[pl.BlockSpec(memory_space=pltpu.SMEM),
                  pl.BlockSpec((1, WINDOW, B_QD), lambda i, j: (i, j, 0)),
                  pl.BlockSpec((1, WINDOW, 2 * B_KVD), lambda i, j: (i, jnp.maximum(j - 1, 0), 0)),
                  pl.BlockSpec((1, WINDOW, 2 * B_KVD), lambda i, j: (i, j, 0))],
        out_specs=pl.BlockSpec((1, WINDOW, B_QD), lambda i, j: (i, j, 0)),
        out_shape=jax.ShapeDtypeStruct((b, l, B_QD), BF16),
        compiler_params=_params(2),
        name="attn_prompt",
    )(sinks, q, kv, kv)


def _attn_sample_kernel(sink_ref, q_ref, kv_ref, ck_ref, cv_ref, o_ref, *, bb, t, nbuf):
    rows = B_GROUP * t
    qi = lax.broadcasted_iota(jnp.int32, (rows, nbuf + t), 0) % t
    ki = lax.broadcasted_iota(jnp.int32, (rows, nbuf + t), 1)
    rel = qi + nbuf - ki
    mask = (rel >= 0) & (rel < WINDOW)
    grp = lax.broadcasted_iota(jnp.int32, (rows, 1), 0) // t
    items = [(b, j) for b in range(bb) for j in range(B_KV_HEADS)]
    sink_col = {}
    for j in range(B_KV_HEADS):
        col = jnp.full((rows, 1), sink_ref[j * B_GROUP + B_GROUP - 1], F32)
        for g in range(B_GROUP - 2, -1, -1):
            col = jnp.where(grp == g, sink_ref[j * B_GROUP + g], col)
        sink_col[j] = col
    keys, vals = {}, {}
    for b in range(bb):
        kv = kv_ref[b]
        keys[b] = jnp.concatenate([ck_ref[b], kv[:, :B_KVD]], axis=0).astype(BF16)
        vals[b] = jnp.concatenate([cv_ref[b], kv[:, B_KVD:]], axis=0).astype(BF16)
    s = {}
    for b, j in items:
        qs = jnp.concatenate(
            [q_ref[b, :, (j * B_GROUP + g) * B_HEAD_DIM:(j * B_GROUP + g + 1) * B_HEAD_DIM]
             for g in range(B_GROUP)], axis=0).astype(BF16)
        s[b, j] = jnp.where(mask, _dot_nt(qs, keys[b][:, j * B_HEAD_DIM:(j + 1) * B_HEAD_DIM]), NEG_INF)
    p, den = {}, {}
    for b, j in items:
        mx = jnp.maximum(jnp.max(s[b, j], axis=-1, keepdims=True), sink_col[j])
        p[b, j] = jnp.exp(s[b, j] - mx)
        den[b, j] = jnp.sum(p[b, j], axis=-1, keepdims=True) + jnp.exp(sink_col[j] - mx)
    o = {}
    for b, j in items:
        o[b, j] = _dot(p[b, j].astype(BF16), vals[b][:, j * B_HEAD_DIM:(j + 1) * B_HEAD_DIM]) / den[b, j]
    for b in range(bb):
        o_ref[b] = jnp.concatenate(
            [o[b, j][g * t:(g + 1) * t, :] for j in range(B_KV_HEADS) for g in range(B_GROUP)],
            axis=1).astype(o_ref.dtype)


def _attn_sample(q, kv, ck, cv, sinks, bb):
    b, t, _ = q.shape
    nbuf = ck.shape[1]
    per_b = lambda r, n: pl.BlockSpec((bb, r, n), lambda i: (i, 0, 0))
    return pl.pallas_call(
        functools.partial(_attn_sample_kernel, bb=bb, t=t, nbuf=nbuf),
        grid=(b // bb,),
        in_specs=[pl.BlockSpec(memory_space=pltpu.SMEM),
                  per_b(t, B_QD), per_b(t, 2 * B_KVD), per_b(nbuf, B_KVD), per_b(nbuf, B_KVD)],
        out_specs=per_b(t, B_QD),
        out_shape=jax.ShapeDtypeStruct((b, t, B_QD), BF16),
        compiler_params=_params(1),
        name="attn_sample",
    )(sinks, q, kv, ck, cv)


def _pad_lanes(v):
    return jnp.pad(v, ((0, 0), (0, LANES - v.shape[1])))


def _prepare(p):
    w_in = p["a_w_in"][0]
    n_gate = A_CONV_CH + A_VD
    row = lambda v: v.reshape(1, -1).astype(F32)

    def post_common(i, w_mix):
        return (w_mix.astype(BF16), row(p["ffn_norm"][i]), p["ffn_w_gu"][i].astype(BF16),
                p["ffn_w_down"][i].astype(BF16), row(p["ple_norm"][i]),
                p["ple_w_gate"][i].astype(BF16), p["ple_w_proj"][i].astype(BF16))

    return {
        "a_norm": row(p["a_norm"][0]),
        "w_qkv": w_in[:, :A_CONV_CH].astype(BF16),
        "w_z": w_in[:, A_CONV_CH:n_gate].astype(BF16),
        "w_ab": jnp.concatenate([_pad_lanes(w_in[:, n_gate:n_gate + A_HEADS]),
                                 _pad_lanes(w_in[:, n_gate + A_HEADS:])], axis=1).astype(BF16),
        "conv_w": p["a_conv_w"][0],
        "gate_prm": jnp.concatenate([_pad_lanes(row(p["a_a_log"][0])),
                                     _pad_lanes(row(p["a_dt_bias"][0]))], axis=0),
        "out_g": row(p["a_out_norm"][0]),
        "post0": post_common(0, p["a_w_out"][0]),
        "tail0": (row(p["kv_norm"]), p["kv_w"].astype(BF16), row(p["b_norm"][0]), p["b_w_q"][0].astype(BF16)),
        "post1": post_common(1, p["b_w_o"][0]),
        "tail1": (row(p["final_norm"]),),
        "sinks": p["b_sinks"][0].astype(F32),
    }


def _tiles(b, l, prompt):
    m = b * l
    c = min(CHUNK, l)
    if prompt:
        return dict(tm=min(512, m), pc_bb=1, pc_tl=min(512, l), dl_tl=min(256, l), c=c, at_bb=1)
    return dict(tm=min(512, m), pc_bb=min(32, b), pc_tl=l, dl_tl=min(8 * l, m), c=c, at_bb=min(8, b))


def _trunk(x, pe, s_init, c_init, k_buf, v_buf, w):
    b, l, _ = x.shape
    m = b * l
    prompt = k_buf is None
    ts = _tiles(b, l, prompt)
    act_dtype = BF16 if prompt else F32
    qkv, zg, gb, c_new = _proj_conv(x, c_init, w, ts["pc_bb"], ts["pc_tl"], act_dtype)
    if prompt:
        o, s_new = _delta(qkv, zg, gb, s_init, w["out_g"], ts["dl_tl"], ts["c"], True)
    else:
        grp = lambda a: a.reshape(m // ts["dl_tl"], ts["dl_tl"], a.shape[-1])
        o, s_new = _delta(grp(qkv), grp(zg), grp(gb), s_init, w["out_g"], ts["dl_tl"], ts["c"], False)
    x2d = x.reshape(m, D_MODEL)
    pe2d = pe.reshape(pe.shape[0], m, PLE_DIM)
    h, q, kv = _post(o.reshape(m, -1), x2d, pe2d, 0, w["post0"], w["tail0"], ts["tm"], False, act_dtype)
    q3 = q.reshape(b, l, -1)
    kv3 = kv.reshape(b, l, -1)
    k_sh = kv3[:, :, :B_KVD].reshape(b, l, B_KV_HEADS, B_HEAD_DIM)
    v_sh = kv3[:, :, B_KVD:].reshape(b, l, B_KV_HEADS, B_HEAD_DIM)
    if prompt:
        att = _attn_prompt(q3, kv3, w["sinks"])
        k_win = k_sh[:, -WINDOW:]
        v_win = v_sh[:, -WINDOW:]
    else:
        nbuf = k_buf.shape[1]
        att = _attn_sample(q3, kv3, k_buf.reshape(b, nbuf, B_KVD), v_buf.reshape(b, nbuf, B_KVD),
                           w["sinks"], ts["at_bb"])
        k_win = jnp.concatenate([k_buf, k_sh], axis=1)[:, -nbuf:]
        v_win = jnp.concatenate([v_buf, v_sh], axis=1)[:, -nbuf:]
    (y,) = _post(att.reshape(m, -1), h, pe2d, 1, w["post1"], w["tail1"], ts["tm"], True)
    return y.reshape(b, l, D_MODEL), s_new[None], c_new[None], k_win, v_win


def kernel(x_prompt, x_sample, state_delta, state_conv, cache_k_win, cache_v_win, p_prompt, p_sample, a_norm, a_w_in, a_conv_w, a_a_log, a_dt_bias, a_out_norm, a_w_out, kv_norm, kv_w, b_norm, b_w_q, b_sinks, b_w_o, ffn_norm, ffn_w_gu, ffn_w_down, ple_norm, ple_w_proj, ple_w_gate, final_norm):
    w = _prepare({
        "a_norm": a_norm, "a_w_in": a_w_in, "a_conv_w": a_conv_w, "a_a_log": a_a_log,
        "a_dt_bias": a_dt_bias, "a_out_norm": a_out_norm, "a_w_out": a_w_out,
        "kv_norm": kv_norm, "kv_w": kv_w, "b_norm": b_norm, "b_w_q": b_w_q, "b_sinks": b_sinks,
        "b_w_o": b_w_o, "ffn_norm": ffn_norm, "ffn_w_gu": ffn_w_gu, "ffn_w_down": ffn_w_down,
        "ple_norm": ple_norm, "ple_w_proj": ple_w_proj, "ple_w_gate": ple_w_gate,
        "final_norm": final_norm,
    })
    bp = x_prompt.shape[0]
    s0 = jnp.zeros((bp, A_HEADS, A_DK, A_DV), F32)
    c0 = jnp.zeros((bp, CONV_W - 1, A_CONV_CH), x_prompt.dtype)
    y_p, sd_p, sc_p, kw_p, vw_p = _trunk(x_prompt, p_prompt, s0, c0, None, None, w)
    y_s, sd_s, sc_s, kw_s, vw_s = _trunk(x_sample, p_sample, state_delta[0], state_conv[0],
                                         cache_k_win, cache_v_win, w)
    return (y_p, y_s, sd_p, sd_s, sc_p, sc_s, kw_p, kw_s, vw_p, vw_s)
```

```python
import functools

import jax
import jax.numpy as jnp
from jax import lax
from jax.experimental import pallas as pl
from jax.experimental.pallas import tpu as pltpu

F32 = jnp.float32
BF16 = jnp.bfloat16

D_MODEL = 1024
A_HEADS = 8
A_DK = 128
A_DV = 128
A_KD = A_HEADS * A_DK
A_VD = A_HEADS * A_DV
A_CONV_CH = 2 * A_KD + A_VD
CONV_W = 4
CHUNK = 64
B_Q_HEADS = 16
B_KV_HEADS = 4
B_HEAD_DIM = 64
B_GROUP = B_Q_HEADS // B_KV_HEADS
B_QD = B_Q_HEADS * B_HEAD_DIM
B_KVD = B_KV_HEADS * B_HEAD_DIM
WINDOW = 128
FFN_HIDDEN = 2816
PLE_DIM = 256
EPS = 1e-6
L2_EPS = 1e-6
NEG_INF = -1e30

LANES = 128
SUBLANES = 8
VMEM_LIMIT_BYTES = 56 * 1024 * 1024
PROJ_SLAB = 512
WEAVE_EVERY = 3


def _rms(x, g):
    return x * lax.rsqrt(jnp.mean(x * x, axis=-1, keepdims=True) + EPS) * g


def _sigmoid(x):
    return 1.0 / (1.0 + jnp.exp(-x))


def _silu(x):
    return x * _sigmoid(x)


def _softplus(x):
    return jnp.maximum(x, 0.0) + jnp.log1p(jnp.exp(-jnp.abs(x)))


def _dot(a, b):
    return jnp.dot(a, b, preferred_element_type=F32)


def _dot_nt(a, b):
    return lax.dot_general(a, b, (((1,), (1,)), ((), ())), preferred_element_type=F32)


def _dot_tn(a, b):
    return lax.dot_general(a, b, (((0,), (0,)), ((), ())), preferred_element_type=F32)


def _hi_lo(x):
    hi = x.astype(BF16)
    return hi, (x - hi.astype(F32)).astype(BF16)


def _split3(x):
    p0 = x.astype(BF16)
    r = x - p0.astype(F32)
    p1 = r.astype(BF16)
    p2 = (r - p1.astype(F32)).astype(BF16)
    return p0, p1, p2


def _const_spec(arr, layer=None):
    if layer is None:
        n = arr.ndim
        return pl.BlockSpec(arr.shape, lambda *_: (0,) * n, pipeline_mode=pl.Buffered(1))
    n = arr.ndim - 1
    return pl.BlockSpec((None,) + arr.shape[1:], lambda *_: (layer,) + (0,) * n,
                        pipeline_mode=pl.Buffered(1))


def _params(n_axes):
    return pltpu.CompilerParams(dimension_semantics=("arbitrary",) * n_axes,
                                vmem_limit_bytes=VMEM_LIMIT_BYTES)


def _run(gen):
    for _ in gen:
        pass


def _weave(main, side, every):
    next(side, None)
    for i, _ in enumerate(main):
        if (i + 1) % every == 0:
            next(side, None)
    _run(side)


def _proj_conv_stages(xn, bb, tl, ext_scr, c_ref, wqkv_ref, wz_ref, wab_ref, cw_ref, gp_ref,
                      qkv_out, zg_out, gb_out):
    hist = CONV_W - 1
    pad = SUBLANES
    for s in range(A_CONV_CH // PROJ_SLAB):
        c0 = s * PROJ_SLAB
        cols = slice(c0, c0 + PROJ_SLAB)
        p = _dot(xn, wqkv_ref[:, cols]).reshape(bb, tl, PROJ_SLAB)
        ext_scr[:, pad:pad + tl, cols] = p
        conv = p * cw_ref[hist:hist + 1, cols]
        for j in range(1, CONV_W):
            conv = conv + ext_scr[:, pad - j:pad - j + tl, cols] * cw_ref[hist - j:hist - j + 1, cols]
        tail = p[:, tl - hist:, :]
        ext_scr[:, pad - hist:pad, cols] = tail
        c_ref[:, :, cols] = tail
        act = _silu(conv)
        if c0 < 2 * A_KD:
            scale = A_DK ** -0.5 if c0 < A_KD else 1.0
            for h in range(PROJ_SLAB // A_DK):
                a = act[:, :, h * A_DK:(h + 1) * A_DK]
                a = a * lax.rsqrt(jnp.sum(a * a, axis=-1, keepdims=True) + L2_EPS) * scale
                qkv_out[:, :, c0 + h * A_DK:c0 + (h + 1) * A_DK] = a.astype(qkv_out.dtype)
        else:
            qkv_out[:, :, cols] = act.astype(qkv_out.dtype)
        yield
    for s in range(A_VD // PROJ_SLAB):
        cols = slice(s * PROJ_SLAB, (s + 1) * PROJ_SLAB)
        zg = _silu(_dot(xn, wz_ref[:, cols]))
        zg_out[:, :, cols] = zg.reshape(bb, tl, PROJ_SLAB).astype(zg_out.dtype)
        yield
    ab = _dot(xn, wab_ref[...])
    g = -jnp.exp(gp_ref[0:1, :]) * _softplus(ab[:, :LANES] + gp_ref[1:2, :])
    beta = _sigmoid(ab[:, LANES:])
    gb_out[...] = jnp.concatenate([g, beta], axis=1).reshape(bb, tl, 2 * LANES)
    yield


def _proj_conv_kernel(x_ref, cb_ref, ng_ref, wqkv_ref, wz_ref, wab_ref, cw_ref, gp_ref,
                      qkv_ref, zg_ref, gb_ref, c_ref, ext_scr, *, bb, tl):
    @pl.when(pl.program_id(1) == 0)
    def _():
        ext_scr[:, SUBLANES - (CONV_W - 1):SUBLANES, :] = cb_ref[...]

    xn = _rms(x_ref[...].reshape(bb * tl, D_MODEL), ng_ref[...]).astype(BF16)
    _run(_proj_conv_stages(xn, bb, tl, ext_scr, c_ref, wqkv_ref, wz_ref, wab_ref, cw_ref, gp_ref,
                           qkv_ref, zg_ref, gb_ref))


def _proj_consts(w):
    return [w["a_norm"], w["w_qkv"], w["w_z"], w["w_ab"], w["conv_w"], w["gate_prm"]]


def _proj_conv(x, cbuf, w, bb, tl, act_dtype):
    b, l, _ = x.shape
    tok = lambda n: pl.BlockSpec((bb, tl, n), lambda i, j: (i, j, 0))
    hist = pl.BlockSpec((bb, CONV_W - 1, A_CONV_CH), lambda i, j: (i, 0, 0))
    consts = _proj_consts(w)
    return pl.pallas_call(
        functools.partial(_proj_conv_kernel, bb=bb, tl=tl),
        grid=(b // bb, l // tl),
        in_specs=[tok(D_MODEL), hist] + [_const_spec(c) for c in consts],
        out_specs=[tok(A_CONV_CH), tok(A_VD), tok(2 * LANES), hist],
        out_shape=[jax.ShapeDtypeStruct((b, l, A_CONV_CH), act_dtype),
                   jax.ShapeDtypeStruct((b, l, A_VD), act_dtype),
                   jax.ShapeDtypeStruct((b, l, 2 * LANES), F32),
                   jax.ShapeDtypeStruct((b, CONV_W - 1, A_CONV_CH), F32)],
        scratch_shapes=[pltpu.VMEM((bb, tl + SUBLANES, A_CONV_CH), F32)],
        compiler_params=_params(2),
        name="proj_conv",
    )(x, cbuf, *consts)


def _delta_scratch(tl, c):
    nc = tl // c
    gp = min(LANES // c, A_HEADS)
    mm_dtype = BF16 if c % 16 == 0 else F32
    return [pltpu.VMEM((nc, A_HEADS, 2 * c, A_DK), mm_dtype),
            pltpu.VMEM((nc, A_HEADS, c, A_DV), F32),
            pltpu.VMEM((nc, A_HEADS, c, A_DK), mm_dtype),
            pltpu.VMEM((nc, A_HEADS // gp, c, gp * c), mm_dtype),
            pltpu.VMEM((tl, A_VD), F32)]


def _delta_stages(qkv_ref, zg_ref, gb_ref, s_in_ref, og_ref, o_ref, s_ref,
                  wq_scr, ub_scr, kd_scr, qk_scr, o_scr, *, tl, c, carry):
    gp = min(LANES // c, A_HEADS)
    npk = A_HEADS // gp
    pw = gp * c
    nc = tl // c
    st = (lambda n, h: (0, h)) if carry else (lambda n, h: (n, h))

    ri = lax.broadcasted_iota(jnp.int32, (c, pw), 0)
    li = lax.broadcasted_iota(jnp.int32, (c, pw), 1)
    cj = li % c
    blk = li // c
    incl = ri >= cj
    strict = ri > cj
    eye = (ri == cj).astype(F32)
    tri = (lax.broadcasted_iota(jnp.int32, (c, c), 0) >= lax.broadcasted_iota(jnp.int32, (c, c), 1))
    tri = tri.astype(F32).astype(BF16)
    wide_blk = lax.broadcasted_iota(jnp.int32, (c, gp * A_DK), 1) // A_DK

    def lanes_of(cols, width):
        sel = blk if width == pw else wide_blk
        out = jnp.broadcast_to(cols[-1], (c, width))
        for i in range(gp - 2, -1, -1):
            out = jnp.where(sel == i, jnp.broadcast_to(cols[i], (c, width)), out)
        return out

    def block_diag(y):
        return jnp.concatenate([jnp.where(blk == i, y, 0.0) for i in range(gp)], axis=0).astype(BF16)

    def block_rows(parts):
        w = parts[0].shape[1]
        wide = jnp.concatenate(parts, axis=1)
        sel = lax.broadcasted_iota(jnp.int32, (c, gp * w), 1) // w
        return jnp.concatenate([jnp.where(sel == i, wide, 0.0) for i in range(gp)], axis=0).astype(BF16)

    items = [(n, p) for n in range(nc) for p in range(npk)]
    gc = []
    for n in range(nc):
        g0, g1, g2 = _split3(gb_ref[0, n * c:(n + 1) * c, :LANES])
        gc.append(_dot(tri, g0) + (_dot(tri, g1) + _dot(tri, g2)))
    decay, a_neg, t_mat = {}, {}, {}
    for n, p in items:
        rows = slice(n * c, (n + 1) * c)
        g_pk = lanes_of([gb_ref[0, rows, p * gp + i:p * gp + i + 1] for i in range(gp)], pw)
        hi, lo = _hi_lo(jnp.where(strict, g_pk, 0.0))
        diff = _dot(tri, hi) + _dot(tri, lo)
        decay[n, p] = jnp.where(incl, jnp.exp(jnp.where(incl, diff, 0.0)), 0.0)
    yield
    for n, p in items:
        rows = slice(n * c, (n + 1) * c)
        kcols = slice(A_KD + p * gp * A_DK, A_KD + (p + 1) * gp * A_DK)
        k_pk = qkv_ref[0, rows, kcols]
        q_pk = qkv_ref[0, rows, p * gp * A_DK:(p + 1) * gp * A_DK]
        beta_w = lanes_of([gb_ref[0, rows, LANES + p * gp + i:LANES + p * gp + i + 1] for i in range(gp)],
                          gp * A_DK)
        k32 = k_pk.astype(F32)
        k_rows = jnp.concatenate([jnp.where(wide_blk == i, k32, 0.0) for i in range(gp)],
                                 axis=0).astype(BF16)
        kk = _dot_nt((k32 * beta_w).astype(BF16), k_rows)
        qk = _dot_nt(q_pk.astype(BF16), k_rows)
        a_neg[n, p] = -jnp.where(strict, kk * decay[n, p], 0.0)
        qk_scr[n, p] = jnp.where(incl, qk * decay[n, p], 0.0).astype(qk_scr.dtype)
    yield
    pw_mat = dict(a_neg)
    for key in items:
        t_mat[key] = eye + a_neg[key]
    m = 2
    while m < c:
        for key in items:
            pw_mat[key] = _dot(pw_mat[key].astype(BF16), block_diag(pw_mat[key]))
        yield
        for key in items:
            t_mat[key] = t_mat[key] + _dot(t_mat[key].astype(BF16), block_diag(pw_mat[key]))
        yield
        m *= 2
    for n, p in items:
        rows = slice(n * c, (n + 1) * c)
        xs = []
        for i in range(gp):
            h = p * gp + i
            k = qkv_ref[0, rows, A_KD + h * A_DK:A_KD + (h + 1) * A_DK].astype(F32)
            q = qkv_ref[0, rows, h * A_DK:(h + 1) * A_DK].astype(F32)
            v = qkv_ref[0, rows, 2 * A_KD + h * A_DV:2 * A_KD + (h + 1) * A_DV].astype(F32)
            gcol = gc[n][:, h:h + 1]
            bcol = gb_ref[0, rows, LANES + h:LANES + h + 1]
            egc = jnp.exp(gcol)
            kbeta = k * bcol
            xs.append(jnp.concatenate([v * bcol, kbeta * egc], axis=1))
            wq_scr[n, h, c:, :] = (q * egc).astype(wq_scr.dtype)
            kd_scr[n, h] = (k * jnp.exp(gc[n][c - 1:c, h:h + 1] - gcol)).astype(kd_scr.dtype)
        sol = _dot(t_mat[n, p].astype(BF16), block_rows(xs))
        for i in range(gp):
            h = p * gp + i
            ub_scr[n, h] = sol[:, i * 2 * A_DV:i * 2 * A_DV + A_DV]
            wq_scr[n, h, :c, :] = sol[:, i * 2 * A_DV + A_DV:(i + 1) * 2 * A_DV].astype(wq_scr.dtype)
    yield

    heads = range(A_HEADS)
    for n in range(nc):
        s = [(s_ref if carry else s_in_ref)[st(n, h)] for h in heads]
        s16 = [s[h].astype(BF16) for h in heads]
        wq = [_dot(wq_scr[n, h].astype(BF16), s16[h]) for h in heads]
        yield
        u = [ub_scr[n, h] - wq[h][:c] for h in heads]
        for p in range(npk):
            hs = [p * gp + i for i in range(gp)]
            o_pk = _dot(qk_scr[n, p].astype(BF16), block_rows([u[h] for h in hs]))
            o_scr[n * c:(n + 1) * c, p * gp * A_DV:(p + 1) * gp * A_DV] = (
                jnp.concatenate([wq[h][c:] for h in hs], axis=1) + o_pk)
        ku = [_dot_tn(kd_scr[n, h].astype(BF16), u[h].astype(BF16)) for h in heads]
        for h in heads:
            s_ref[st(n, h)] = s[h] * jnp.exp(gc[n][c - 1:c, h:h + 1]) + ku[h]
        yield

    for h in heads:
        vs = slice(h * A_DV, (h + 1) * A_DV)
        oh = o_scr[:, vs]
        oh = oh * lax.rsqrt(jnp.mean(oh * oh, axis=-1, keepdims=True) + EPS) * og_ref[...]
        o_ref[0, :, vs] = (oh * zg_ref[0, :, vs].astype(F32)).astype(o_ref.dtype)


def _delta_kernel(qkv_ref, zg_ref, gb_ref, s0_ref, og_ref, o_ref, s_ref, *scratch, tl, c, carry):
    if carry:
        @pl.when(pl.program_id(1) == 0)
        def _():
            s_ref[...] = s0_ref[...]
    _run(_delta_stages(qkv_ref, zg_ref, gb_ref, s0_ref, og_ref, o_ref, s_ref, *scratch,
                       tl=tl, c=c, carry=carry))


def _delta(qkv, zg, gb, s0, out_g, tl, c, carry):
    b, l, _ = qkv.shape
    assert carry or l == tl
    tok = lambda n: pl.BlockSpec((1, tl, n), lambda i, j: (i, j, 0))
    state = pl.BlockSpec((1 if carry else tl // c, A_HEADS, A_DK, A_DV), lambda i, j: (i, 0, 0, 0))
    return pl.pallas_call(
        functools.partial(_delta_kernel, tl=tl, c=c, carry=carry),
        grid=(b, l // tl),
        in_specs=[tok(A_CONV_CH), tok(A_VD), tok(2 * LANES), state, _const_spec(out_g)],
        out_specs=[tok(A_VD), state],
        out_shape=[jax.ShapeDtypeStruct((b, l, A_VD), BF16), jax.ShapeDtypeStruct(s0.shape, F32)],
        scratch_shapes=_delta_scratch(tl, c),
        compiler_params=_params(2),
        name="delta",
    )(qkv, zg, gb, s0, out_g)


def _mixer_a_kernel(x_ref, cb_ref, s0_ref, ng_ref, wqkv_ref, wz_ref, wab_ref, cw_ref, gp_ref, og_ref,
                    o_ref, s_ref, c_ref, ext_scr, qkv_a, zg_a, gb_a, qkv_b, zg_b, gb_b, *scratch,
                    tl, c, nt, n_tiles):
    k = pl.program_id(0)
    tile_p = jnp.minimum(k, n_tiles - 1)
    tile_d = jnp.maximum(k - 1, 0)

    @pl.when(k == 0)
    def _():
        qkv_b[...] = jnp.zeros_like(qkv_b)
        zg_b[...] = jnp.zeros_like(zg_b)
        gb_b[...] = jnp.zeros_like(gb_b)

    @pl.when(tile_p % nt == 0)
    def _():
        ext_scr[:, SUBLANES - (CONV_W - 1):SUBLANES, :] = cb_ref[...]

    @pl.when(tile_d % nt == 0)
    def _():
        s_ref[...] = s0_ref[...]

    def body(write, read):
        xn = _rms(x_ref[0], ng_ref[...]).astype(BF16)
        proj = _proj_conv_stages(xn, 1, tl, ext_scr, c_ref, wqkv_ref, wz_ref, wab_ref, cw_ref, gp_ref, *write)
        delta = _delta_stages(*read, s0_ref, og_ref, o_ref, s_ref, *scratch, tl=tl, c=c, carry=True)
        _weave(delta, proj, WEAVE_EVERY)

    @pl.when(k % 2 == 0)
    def _():
        body((qkv_a, zg_a, gb_a), (qkv_b, zg_b, gb_b))

    @pl.when(k % 2 == 1)
    def _():
        body((qkv_b, zg_b, gb_b), (qkv_a, zg_a, gb_a))


def _mixer_a(x, cbuf, s0, w, tl, c):
    b, l, _ = x.shape
    nt = l // tl
    n_tiles = b * nt
    p_idx = lambda k: jnp.minimum(k, n_tiles - 1)
    d_idx = lambda k: jnp.maximum(k - 1, 0)
    consts = _proj_consts(w) + [w["out_g"]]
    hand_off = [pltpu.VMEM((1, tl, A_CONV_CH), BF16), pltpu.VMEM((1, tl, A_VD), BF16),
                pltpu.VMEM((1, tl, 2 * LANES), F32)]
    return pl.pallas_call(
        functools.partial(_mixer_a_kernel, tl=tl, c=c, nt=nt, n_tiles=n_tiles),
        grid=(n_tiles + 1,),
        in_specs=[pl.BlockSpec((1, tl, D_MODEL), lambda k: (p_idx(k) // nt, p_idx(k) % nt, 0)),
                  pl.BlockSpec((1, CONV_W - 1, A_CONV_CH), lambda k: (p_idx(k) // nt, 0, 0)),
                  pl.BlockSpec((1, A_HEADS, A_DK, A_DV), lambda k: (d_idx(k) // nt, 0, 0, 0))]
                 + [_const_spec(cst) for cst in consts],
        out_specs=[pl.BlockSpec((1, tl, A_VD), lambda k: (d_idx(k) // nt, d_idx(k) % nt, 0)),
                   pl.BlockSpec((1, A_HEADS, A_DK, A_DV), lambda k: (d_idx(k) // nt, 0, 0, 0)),
                   pl.BlockSpec((1, CONV_W - 1, A_CONV_CH), lambda k: (p_idx(k) // nt, 0, 0))],
        out_shape=[jax.ShapeDtypeStruct((b, l, A_VD), BF16),
                   jax.ShapeDtypeStruct((b, A_HEADS, A_DK, A_DV), F32),
                   jax.ShapeDtypeStruct((b, CONV_W - 1, A_CONV_CH), F32)],
        scratch_shapes=[pltpu.VMEM((1, tl + SUBLANES, A_CONV_CH), F32)] + hand_off + hand_off
                       + _delta_scratch(tl, c),
        compiler_params=_params(1),
        name="mixer_a",
    )(x, cbuf, s0, *consts)


def _post_body(mix_ref, res_ref, pe_ref, wmix_ref, fg_ref, wgu_ref, wd_ref,
               pg_ref, wpg_ref, wpp_ref, n_chunks):
    h = res_ref[...] + _dot(mix_ref[...], wmix_ref[...])
    xn = _rms(h, fg_ref[...]).astype(BF16)
    fc = FFN_HIDDEN // n_chunks
    acc = h
    for i in range(n_chunks):
        gt = _dot(xn, wgu_ref[:, i * fc:(i + 1) * fc])
        up = _dot(xn, wgu_ref[:, FFN_HIDDEN + i * fc:FFN_HIDDEN + (i + 1) * fc])
        acc = acc + _dot((_silu(gt) * up).astype(BF16), wd_ref[i * fc:(i + 1) * fc, :])
    h2 = acc
    gate = _sigmoid(_dot(_rms(h2, pg_ref[...]).astype(BF16), wpg_ref[...]))
    return h2 + _dot(pe_ref[...].astype(BF16), wpp_ref[...]) * gate


def _post_mid_kernel(mix_ref, res_ref, pe_ref, wmix_ref, fg_ref, wgu_ref, wd_ref,
                     pg_ref, wpg_ref, wpp_ref, kvn_ref, kvw_ref, bn_ref, wq_ref,
                     h_ref, q_ref, kv_ref, *, n_chunks):
    h3 = _post_body(mix_ref, res_ref, pe_ref, wmix_ref, fg_ref, wgu_ref, wd_ref,
                    pg_ref, wpg_ref, wpp_ref, n_chunks)
    h_ref[...] = h3
    kv_ref[...] = _dot(_rms(h3, kvn_ref[...]).astype(BF16), kvw_ref[...])
    q = _dot(_rms(h3, bn_ref[...]).astype(BF16), wq_ref[...]) * (B_HEAD_DIM ** -0.5)
    q_ref[...] = q.astype(q_ref.dtype)


def _post_last_kernel(mix_ref, res_ref, pe_ref, wmix_ref, fg_ref, wgu_ref, wd_ref,
                      pg_ref, wpg_ref, wpp_ref, fn_ref, y_ref, *, n_chunks):
    h3 = _post_body(mix_ref, res_ref, pe_ref, wmix_ref, fg_ref, wgu_ref, wd_ref,
                    pg_ref, wpg_ref, wpp_ref, n_chunks)
    y_ref[...] = _rms(h3, fn_ref[...])


def _post(mix, res, pe, layer, common, tail, tm, last, q_dtype=BF16):
    m = res.shape[0]
    row = lambda n: pl.BlockSpec((tm, n), lambda i: (i, 0))
    weights = list(common) + list(tail)
    pe_spec = pl.BlockSpec((None, tm, PLE_DIM), lambda i: (layer, i, 0))
    in_specs = [row(mix.shape[1]), row(D_MODEL), pe_spec] + [_const_spec(a, lyr) for a, lyr in weights]
    if last:
        kern = _post_last_kernel
        out_specs = [row(D_MODEL)]
        out_shape = [jax.ShapeDtypeStruct((m, D_MODEL), F32)]
    else:
        kern = _post_mid_kernel
        out_specs = [row(D_MODEL), row(B_QD), row(2 * B_KVD)]
        out_shape = [jax.ShapeDtypeStruct((m, D_MODEL), F32),
                     jax.ShapeDtypeStruct((m, B_QD), q_dtype),
                     jax.ShapeDtypeStruct((m, 2 * B_KVD), F32)]
    return pl.pallas_call(
        functools.partial(kern, n_chunks=2),
        grid=(m // tm,),
        in_specs=in_specs,
        out_specs=out_specs,
        out_shape=out_shape,
        compiler_params=_params(1),
        name="post_last" if last else "post_mid",
    )(mix, res, pe, *[a for a, _ in weights])


def _attn_prompt_kernel(sink_ref, q_ref, kvp_ref, kvc_ref, o_ref):
    i = pl.program_id(1)
    rows = B_GROUP * WINDOW
    k2 = jnp.concatenate([kvc_ref[0, :, :B_KVD], kvp_ref[0, :, :B_KVD]], axis=0).astype(BF16)
    v2 = jnp.concatenate([kvc_ref[0, :, B_KVD:], kvp_ref[0, :, B_KVD:]], axis=0).astype(BF16)
    qi = lax.broadcasted_iota(jnp.int32, (rows, WINDOW), 0) % WINDOW
    ci = lax.broadcasted_iota(jnp.int32, (rows, WINDOW), 1)
    own = ci <= qi
    prev_ok = (ci > qi) & ((i - 1) * WINDOW + ci >= 0)
    grp = lax.broadcasted_iota(jnp.int32, (rows, 1), 0) // WINDOW
    kv_heads = range(B_KV_HEADS)
    sc = []
    for j in kv_heads:
        qs = jnp.concatenate(
            [q_ref[0, :, (j * B_GROUP + g) * B_HEAD_DIM:(j * B_GROUP + g + 1) * B_HEAD_DIM]
             for g in range(B_GROUP)], axis=0)
        sc.append(_dot_nt(qs, k2[:, j * B_HEAD_DIM:(j + 1) * B_HEAD_DIM]))
    pv = []
    for j in kv_heads:
        sink = jnp.full((rows, 1), sink_ref[j * B_GROUP + B_GROUP - 1], F32)
        for g in range(B_GROUP - 2, -1, -1):
            sink = jnp.where(grp == g, sink_ref[j * B_GROUP + g], sink)
        s = jnp.where(own, sc[j][:, :WINDOW], jnp.where(prev_ok, sc[j][:, WINDOW:], NEG_INF))
        mx = jnp.maximum(jnp.max(s, axis=-1, keepdims=True), sink)
        p = jnp.exp(s - mx)
        den = jnp.sum(p, axis=-1, keepdims=True) + jnp.exp(sink - mx)
        p2 = jnp.concatenate([jnp.where(own, p, 0.0), jnp.where(own, 0.0, p)], axis=1).astype(BF16)
        pv.append((p2, den))
    outs = []
    for j in kv_heads:
        o = _dot(pv[j][0], v2[:, j * B_HEAD_DIM:(j + 1) * B_HEAD_DIM]) / pv[j][1]
        outs += [o[g * WINDOW:(g + 1) * WINDOW, :] for g in range(B_GROUP)]
    o_ref[0] = jnp.concatenate(outs, axis=1).astype(o_ref.dtype)


def _attn_prompt(q, kv, sinks):
    b, l, _ = q.shape
    nb = l // WINDOW
    return pl.pallas_call(
        _attn_prompt_kernel,
        grid=(b, nb),
        in_specs=[pl.BlockSpec(memory_space=pltpu.SMEM),
                  pl.BlockSpec((1, WINDOW, B_QD), lambda i, j: (i, j, 0)),
                  pl.BlockSpec((1, WINDOW, 2 * B_KVD), lambda i, j: (i, jnp.maximum(j - 1, 0), 0)),
                  pl.BlockSpec((1, WINDOW, 2 * B_KVD), lambda i, j: (i, j, 0))],
        out_specs=pl.BlockSpec((1, WINDOW, B_QD), lambda i, j: (i, j, 0)),
        out_shape=jax.ShapeDtypeStruct((b, l, B_QD), BF16),
        compiler_params=_params(2),
        name="attn_prompt",
    )(sinks, q, kv, kv)


def _attn_sample_kernel(sink_ref, q_ref, kv_ref, ck_ref, cv_ref, o_ref, *, bb, t, nbuf):
    rows = B_GROUP * t
    qi = lax.broadcasted_iota(jnp.int32, (rows, nbuf + t), 0) % t
    ki = lax.broadcasted_iota(jnp.int32, (rows, nbuf + t), 1)
    rel = qi + nbuf - ki
    mask = (rel >= 0) & (rel < WINDOW)
    grp = lax.broadcasted_iota(jnp.int32, (rows, 1), 0) // t
    items = [(b, j) for b in range(bb) for j in range(B_KV_HEADS)]
    sink_col = {}
    for j in range(B_KV_HEADS):
        col = jnp.full((rows, 1), sink_ref[j * B_GROUP + B_GROUP - 1], F32)
        for g in range(B_GROUP - 2, -1, -1):
            col = jnp.where(grp == g, sink_ref[j * B_GROUP + g], col)
        sink_col[j] = col
    keys, vals = {}, {}
    for b in range(bb):
        kv = kv_ref[b]
        keys[b] = jnp.concatenate([ck_ref[b], kv[:, :B_KVD]], axis=0).astype(BF16)
        vals[b] = jnp.concatenate([cv_ref[b], kv[:, B_KVD:]], axis=0).astype(BF16)
    s = {}
    for b, j in items:
        qs = jnp.concatenate(
            [q_ref[b, :, (j * B_GROUP + g) * B_HEAD_DIM:(j * B_GROUP + g + 1) * B_HEAD_DIM]
             for g in range(B_GROUP)], axis=0).astype(BF16)
        s[b, j] = jnp.where(mask, _dot_nt(qs, keys[b][:, j * B_HEAD_DIM:(j + 1) * B_HEAD_DIM]), NEG_INF)
    p, den = {}, {}
    for b, j in items:
        mx = jnp.maximum(jnp.max(s[b, j], axis=-1, keepdims=True), sink_col[j])
        p[b, j] = jnp.exp(s[b, j] - mx)
        den[b, j] = jnp.sum(p[b, j], axis=-1, keepdims=True) + jnp.exp(sink_col[j] - mx)
    o = {}
    for b, j in items:
        o[b, j] = _dot(p[b, j].astype(BF16), vals[b][:, j * B_HEAD_DIM:(j + 1) * B_HEAD_DIM]) / den[b, j]
    for b in range(bb):
        o_ref[b] = jnp.concatenate(
            [o[b, j][g * t:(g + 1) * t, :] for j in range(B_KV_HEADS) for g in range(B_GROUP)],
            axis=1).astype(o_ref.dtype)


def _attn_sample(q, kv, ck, cv, sinks, bb):
    b, t, _ = q.shape
    nbuf = ck.shape[1]
    per_b = lambda r, n: pl.BlockSpec((bb, r, n), lambda i: (i, 0, 0))
    return pl.pallas_call(
        functools.partial(_attn_sample_kernel, bb=bb, t=t, nbuf=nbuf),
        grid=(b // bb,),
        in_specs=[pl.BlockSpec(memory_space=pltpu.SMEM),
                  per_b(t, B_QD), per_b(t, 2 * B_KVD), per_b(nbuf, B_KVD), per_b(nbuf, B_KVD)],
        out_specs=per_b(t, B_QD),
        out_shape=jax.ShapeDtypeStruct((b, t, B_QD), BF16),
        compiler_params=_params(1),
        name="attn_sample",
    )(sinks, q, kv, ck, cv)


def _pad_lanes(v):
    return jnp.pad(v, ((0, 0), (0, LANES - v.shape[1])))


def _prepare(p):
    w_in = p["a_w_in"][0]
    n_gate = A_CONV_CH + A_VD
    row = lambda v: v.reshape(1, -1).astype(F32)
    gu, down = p["ffn_w_gu"].astype(BF16), p["ffn_w_down"].astype(BF16)
    ple_gate, ple_proj = p["ple_w_gate"].astype(BF16), p["ple_w_proj"].astype(BF16)

    def post_common(i, w_mix):
        return [(w_mix.astype(BF16), None), (row(p["ffn_norm"][i]), None), (gu, i), (down, i),
                (row(p["ple_norm"][i]), None), (ple_gate, i), (ple_proj, i)]

    return {
        "a_norm": row(p["a_norm"][0]),
        "w_qkv": w_in[:, :A_CONV_CH].astype(BF16),
        "w_z": w_in[:, A_CONV_CH:n_gate].astype(BF16),
        "w_ab": jnp.concatenate([_pad_lanes(w_in[:, n_gate:n_gate + A_HEADS]),
                                 _pad_lanes(w_in[:, n_gate + A_HEADS:])], axis=1).astype(BF16),
        "conv_w": p["a_conv_w"][0],
        "gate_prm": jnp.concatenate([_pad_lanes(row(p["a_a_log"][0])),
                                     _pad_lanes(row(p["a_dt_bias"][0]))], axis=0),
        "out_g": row(p["a_out_norm"][0]),
        "post0": post_common(0, p["a_w_out"][0]),
        "tail0": [(row(p["kv_norm"]), None), (p["kv_w"].astype(BF16), None),
                  (row(p["b_norm"][0]), None), (p["b_w_q"][0].astype(BF16), None)],
        "post1": post_common(1, p["b_w_o"][0]),
        "tail1": [(row(p["final_norm"]), None)],
        "sinks": p["b_sinks"][0].astype(F32),
    }


def _tiles(b, l, prompt):
    m = b * l
    c = min(CHUNK, l)
    if prompt:
        return dict(tm=min(512, m), mx_tl=min(512, l), c=c)
    return dict(tm=min(512, m), pc_bb=min(32, b), dl_tl=min(8 * l, m), c=c, at_bb=min(8, b))


def _trunk(x, pe, s_init, c_init, k_buf, v_buf, w):
    b, l, _ = x.shape
    m = b * l
    prompt = k_buf is None
    ts = _tiles(b, l, prompt)
    act_dtype = BF16 if prompt else F32
    if prompt:
        o, s_new, c_new = _mixer_a(x, c_init, s_init, w, ts["mx_tl"], ts["c"])
    else:
        qkv, zg, gb, c_new = _proj_conv(x, c_init, w, ts["pc_bb"], l, act_dtype)
        grp = lambda a: a.reshape(m // ts["dl_tl"], ts["dl_tl"], a.shape[-1])
        o, s_new = _delta(grp(qkv), grp(zg), grp(gb), s_init, w["out_g"], ts["dl_tl"], ts["c"], False)
    x2d = x.reshape(m, D_MODEL)
    pe2d = pe.reshape(pe.shape[0], m, PLE_DIM)
    h, q, kv = _post(o.reshape(m, -1), x2d, pe2d, 0, w["post0"], w["tail0"], ts["tm"], False, act_dtype)
    q3 = q.reshape(b, l, -1)
    kv3 = kv.reshape(b, l, -1)
    heads = lambda a: a.reshape(a.shape[0], a.shape[1], B_KV_HEADS, B_HEAD_DIM)
    if prompt:
        att = _attn_prompt(q3, kv3, w["sinks"])
        last = kv3[:, l - WINDOW:]
        k_win, v_win = heads(last[:, :, :B_KVD]), heads(last[:, :, B_KVD:])
    else:
        nbuf = k_buf.shape[1]
        att = _attn_sample(q3, kv3, k_buf.reshape(b, nbuf, B_KVD), v_buf.reshape(b, nbuf, B_KVD),
                           w["sinks"], ts["at_bb"])
        k_win = jnp.concatenate([k_buf, heads(kv3[:, :, :B_KVD])], axis=1)[:, -nbuf:]
        v_win = jnp.concatenate([v_buf, heads(kv3[:, :, B_KVD:])], axis=1)[:, -nbuf:]
    (y,) = _post(att.reshape(m, -1), h, pe2d, 1, w["post1"], w["tail1"], ts["tm"], True)
    return y.reshape(b, l, D_MODEL), s_new[None], c_new[None], k_win, v_win


def kernel(x_prompt, x_sample, state_delta, state_conv, cache_k_win, cache_v_win, p_prompt, p_sample, a_norm, a_w_in, a_conv_w, a_a_log, a_dt_bias, a_out_norm, a_w_out, kv_norm, kv_w, b_norm, b_w_q, b_sinks, b_w_o, ffn_norm, ffn_w_gu, ffn_w_down, ple_norm, ple_w_proj, ple_w_gate, final_norm):
    w = _prepare({
        "a_norm": a_norm, "a_w_in": a_w_in, "a_conv_w": a_conv_w, "a_a_log": a_a_log,
        "a_dt_bias": a_dt_bias, "a_out_norm": a_out_norm, "a_w_out": a_w_out,
        "kv_norm": kv_norm, "kv_w": kv_w, "b_norm": b_norm, "b_w_q": b_w_q, "b_sinks": b_sinks,
        "b_w_o": b_w_o, "ffn_norm": ffn_norm, "ffn_w_gu": ffn_w_gu, "ffn_w_down": ffn_w_down,
        "ple_norm": ple_norm, "ple_w_proj": ple_w_proj, "ple_w_gate": ple_w_gate,
        "final_norm": final_norm,
    })
    bp = x_prompt.shape[0]
    s0 = jnp.zeros((bp, A_HEADS, A_DK, A_DV), F32)
    c0 = jnp.zeros((bp, CONV_W - 1, A_CONV_CH), x_prompt.dtype)
    y_p, sd_p, sc_p, kw_p, vw_p = _trunk(x_prompt, p_prompt, s0, c0, None, None, w)
    y_s, sd_s, sc_s, kw_s, vw_s = _trunk(x_sample, p_sample, state_delta[0], state_conv[0],
                                         cache_k_win, cache_v_win, w)
    return (y_p, y_s, sd_p, sd_s, sc_p, sc_s, kw_p, kw_s, vw_p, vw_s)
```

```python
import functools

import jax
import jax.numpy as jnp
from jax import lax
from jax.experimental import pallas as pl
from jax.experimental.pallas import tpu as pltpu

F32 = jnp.float32
BF16 = jnp.bfloat16

D_MODEL = 1024
A_HEADS = 8
A_DK = 128
A_DV = 128
A_KD = A_HEADS * A_DK
A_VD = A_HEADS * A_DV
A_CONV_CH = 2 * A_KD + A_VD
CONV_W = 4
CHUNK = 64
B_Q_HEADS = 16
B_KV_HEADS = 4
B_HEAD_DIM = 64
B_GROUP = B_Q_HEADS // B_KV_HEADS
B_QD = B_Q_HEADS * B_HEAD_DIM
B_KVD = B_KV_HEADS * B_HEAD_DIM
WINDOW = 128
FFN_HIDDEN = 2816
PLE_DIM = 256
EPS = 1e-6
L2_EPS = 1e-6
NEG_INF = -1e30

LANES = 128
SUBLANES = 8
VMEM_LIMIT_BYTES = 56 * 1024 * 1024
PROJ_SLAB = 512
WEAVE_EVERY = 4
MIXER_S1_GROUP = 4
FFN_CHUNKS = 2


def _rms(x, g):
    return x * lax.rsqrt(jnp.mean(x * x, axis=-1, keepdims=True) + EPS) * g


def _sigmoid(x):
    return 1.0 / (1.0 + jnp.exp(-x))


def _silu(x):
    return x * _sigmoid(x)


def _softplus(x):
    return jnp.maximum(x, 0.0) + jnp.log1p(jnp.exp(-jnp.abs(x)))


def _dot(a, b):
    return jnp.dot(a, b, preferred_element_type=F32)


def _dot_nt(a, b):
    return lax.dot_general(a, b, (((1,), (1,)), ((), ())), preferred_element_type=F32)


def _dot_tn(a, b):
    return lax.dot_general(a, b, (((0,), (0,)), ((), ())), preferred_element_type=F32)


def _hi_lo(x):
    hi = x.astype(BF16)
    return hi, (x - hi.astype(F32)).astype(BF16)


def _split3(x):
    p0 = x.astype(BF16)
    r = x - p0.astype(F32)
    p1 = r.astype(BF16)
    p2 = (r - p1.astype(F32)).astype(BF16)
    return p0, p1, p2


def _const_spec(arr, layer=None):
    if layer is None:
        n = arr.ndim
        return pl.BlockSpec(arr.shape, lambda *_: (0,) * n, pipeline_mode=pl.Buffered(1))
    n = arr.ndim - 1
    return pl.BlockSpec((None,) + arr.shape[1:], lambda *_: (layer,) + (0,) * n,
                        pipeline_mode=pl.Buffered(1))


def _params(n_axes):
    return pltpu.CompilerParams(dimension_semantics=("arbitrary",) * n_axes,
                                vmem_limit_bytes=VMEM_LIMIT_BYTES)


def _run(gen):
    for _ in gen:
        pass


def _weave(main, side, every):
    next(side, None)
    for i, _ in enumerate(main):
        if (i + 1) % every == 0:
            next(side, None)
    _run(side)


def _proj_conv_stages(xn, bb, tl, ext_scr, c_ref, wqkv_ref, wz_ref, wab_ref, cw_ref, gp_ref,
                      qkv_out, zg_out, gb_out):
    hist = CONV_W - 1
    pad = SUBLANES
    for s in range(A_CONV_CH // PROJ_SLAB):
        c0 = s * PROJ_SLAB
        cols = slice(c0, c0 + PROJ_SLAB)
        p = _dot(xn, wqkv_ref[:, cols]).reshape(bb, tl, PROJ_SLAB)
        ext_scr[:, pad:pad + tl, cols] = p
        conv = p * cw_ref[hist:hist + 1, cols]
        for j in range(1, CONV_W):
            conv = conv + ext_scr[:, pad - j:pad - j + tl, cols] * cw_ref[hist - j:hist - j + 1, cols]
        tail = p[:, tl - hist:, :]
        ext_scr[:, pad - hist:pad, cols] = tail
        c_ref[:, :, cols] = tail
        act = _silu(conv)
        if c0 < 2 * A_KD:
            scale = A_DK ** -0.5 if c0 < A_KD else 1.0
            for h in range(PROJ_SLAB // A_DK):
                a = act[:, :, h * A_DK:(h + 1) * A_DK]
                a = a * lax.rsqrt(jnp.sum(a * a, axis=-1, keepdims=True) + L2_EPS) * scale
                qkv_out[:, :, c0 + h * A_DK:c0 + (h + 1) * A_DK] = a.astype(qkv_out.dtype)
        else:
            qkv_out[:, :, cols] = act.astype(qkv_out.dtype)
        yield
    for s in range(A_VD // PROJ_SLAB):
        cols = slice(s * PROJ_SLAB, (s + 1) * PROJ_SLAB)
        zg = _silu(_dot(xn, wz_ref[:, cols]))
        zg_out[:, :, cols] = zg.reshape(bb, tl, PROJ_SLAB).astype(zg_out.dtype)
        yield
    ab = _dot(xn, wab_ref[...])
    g = -jnp.exp(gp_ref[0:1, :]) * _softplus(ab[:, :LANES] + gp_ref[1:2, :])
    beta = _sigmoid(ab[:, LANES:])
    gb_out[...] = jnp.concatenate([g, beta], axis=1).reshape(bb, tl, 2 * LANES)
    yield


def _proj_conv_kernel(x_ref, cb_ref, ng_ref, wqkv_ref, wz_ref, wab_ref, cw_ref, gp_ref,
                      qkv_ref, zg_ref, gb_ref, c_ref, ext_scr, *, bb, tl):
    @pl.when(pl.program_id(1) == 0)
    def _():
        ext_scr[:, SUBLANES - (CONV_W - 1):SUBLANES, :] = cb_ref[...]

    xn = _rms(x_ref[...].reshape(bb * tl, D_MODEL), ng_ref[...]).astype(BF16)
    _run(_proj_conv_stages(xn, bb, tl, ext_scr, c_ref, wqkv_ref, wz_ref, wab_ref, cw_ref, gp_ref,
                           qkv_ref, zg_ref, gb_ref))


def _proj_consts(w):
    return [w["a_norm"], w["w_qkv"], w["w_z"], w["w_ab"], w["conv_w"], w["gate_prm"]]


def _proj_conv(x, cbuf, w, bb, tl, act_dtype):
    b, l, _ = x.shape
    tok = lambda n: pl.BlockSpec((bb, tl, n), lambda i, j: (i, j, 0))
    hist = pl.BlockSpec((bb, CONV_W - 1, A_CONV_CH), lambda i, j: (i, 0, 0))
    consts = _proj_consts(w)
    return pl.pallas_call(
        functools.partial(_proj_conv_kernel, bb=bb, tl=tl),
        grid=(b // bb, l // tl),
        in_specs=[tok(D_MODEL), hist] + [_const_spec(c) for c in consts],
        out_specs=[tok(A_CONV_CH), tok(A_VD), tok(2 * LANES), hist],
        out_shape=[jax.ShapeDtypeStruct((b, l, A_CONV_CH), act_dtype),
                   jax.ShapeDtypeStruct((b, l, A_VD), act_dtype),
                   jax.ShapeDtypeStruct((b, l, 2 * LANES), F32),
                   jax.ShapeDtypeStruct((b, CONV_W - 1, A_CONV_CH), F32)],
        scratch_shapes=[pltpu.VMEM((bb, tl + SUBLANES, A_CONV_CH), F32)],
        compiler_params=_params(2),
        name="proj_conv",
    )(x, cbuf, *consts)


def _delta_scratch(tl, c):
    nc = tl // c
    gp = min(LANES // c, A_HEADS)
    mm_dtype = BF16 if c % 16 == 0 else F32
    return [pltpu.VMEM((nc, A_HEADS, 2 * c, A_DK), mm_dtype),
            pltpu.VMEM((nc, A_HEADS, c, A_DV), F32),
            pltpu.VMEM((nc, A_HEADS, c, A_DK), mm_dtype),
            pltpu.VMEM((nc, A_HEADS // gp, c, gp * c), mm_dtype),
            pltpu.VMEM((tl, A_VD), F32)]


def _delta_stages(qkv_ref, zg_ref, gb_ref, s_in_ref, og_ref, o_ref, s_ref,
                  wq_scr, ub_scr, kd_scr, qk_scr, o_scr, *, tl, c, carry, s1_group):
    gp = min(LANES // c, A_HEADS)
    npk = A_HEADS // gp
    pw = gp * c
    nc = tl // c
    st = (lambda n, h: (0, h)) if carry else (lambda n, h: (n, h))

    ri = lax.broadcasted_iota(jnp.int32, (c, pw), 0)
    li = lax.broadcasted_iota(jnp.int32, (c, pw), 1)
    cj = li % c
    blk = li // c
    incl = ri >= cj
    strict = ri > cj
    eye = (ri == cj).astype(F32)
    tri = (lax.broadcasted_iota(jnp.int32, (c, c), 0) >= lax.broadcasted_iota(jnp.int32, (c, c), 1))
    tri = tri.astype(F32).astype(BF16)
    wide_blk = lax.broadcasted_iota(jnp.int32, (c, gp * A_DK), 1) // A_DK

    def lanes_of(cols, width):
        sel = blk if width == pw else wide_blk
        out = jnp.broadcast_to(cols[-1], (c, width))
        for i in range(gp - 2, -1, -1):
            out = jnp.where(sel == i, jnp.broadcast_to(cols[i], (c, width)), out)
        return out

    def block_diag(y):
        return jnp.concatenate([jnp.where(blk == i, y, 0.0) for i in range(gp)], axis=0).astype(BF16)

    def block_rows(parts):
        w = parts[0].shape[1]
        wide = jnp.concatenate(parts, axis=1)
        sel = lax.broadcasted_iota(jnp.int32, (c, gp * w), 1) // w
        return jnp.concatenate([jnp.where(sel == i, wide, 0.0) for i in range(gp)], axis=0).astype(BF16)

    gc = {}
    for g0 in range(0, nc, s1_group):
        chunks = range(g0, min(g0 + s1_group, nc))
        items = [(n, p) for n in chunks for p in range(npk)]
        for n in chunks:
            ga, gb1, gb2 = _split3(gb_ref[0, n * c:(n + 1) * c, :LANES])
            gc[n] = _dot(tri, ga) + (_dot(tri, gb1) + _dot(tri, gb2))
        decay, a_neg, t_mat = {}, {}, {}
        for n, p in items:
            rows = slice(n * c, (n + 1) * c)
            g_pk = lanes_of([gb_ref[0, rows, p * gp + i:p * gp + i + 1] for i in range(gp)], pw)
            hi, lo = _hi_lo(jnp.where(strict, g_pk, 0.0))
            diff = _dot(tri, hi) + _dot(tri, lo)
            decay[n, p] = jnp.where(incl, jnp.exp(jnp.where(incl, diff, 0.0)), 0.0)
        yield
        for n, p in items:
            rows = slice(n * c, (n + 1) * c)
            kcols = slice(A_KD + p * gp * A_DK, A_KD + (p + 1) * gp * A_DK)
            k_pk = qkv_ref[0, rows, kcols]
            q_pk = qkv_ref[0, rows, p * gp * A_DK:(p + 1) * gp * A_DK]
            beta_w = lanes_of([gb_ref[0, rows, LANES + p * gp + i:LANES + p * gp + i + 1] for i in range(gp)],
                              gp * A_DK)
            k32 = k_pk.astype(F32)
            k_rows = jnp.concatenate([jnp.where(wide_blk == i, k32, 0.0) for i in range(gp)],
                                     axis=0).astype(BF16)
            kk = _dot_nt((k32 * beta_w).astype(BF16), k_rows)
            qk = _dot_nt(q_pk.astype(BF16), k_rows)
            a_neg[n, p] = -jnp.where(strict, kk * decay[n, p], 0.0)
            qk_scr[n, p] = jnp.where(incl, qk * decay[n, p], 0.0).astype(qk_scr.dtype)
        yield
        pw_mat = dict(a_neg)
        for key in items:
            t_mat[key] = eye + a_neg[key]
        m = 2
        while m < c:
            for key in items:
                pw_mat[key] = _dot(pw_mat[key].astype(BF16), block_diag(pw_mat[key]))
            yield
            for key in items:
                t_mat[key] = t_mat[key] + _dot(t_mat[key].astype(BF16), block_diag(pw_mat[key]))
            yield
            m *= 2
        for n, p in items:
            rows = slice(n * c, (n + 1) * c)
            xs = []
            for i in range(gp):
                h = p * gp + i
                k = qkv_ref[0, rows, A_KD + h * A_DK:A_KD + (h + 1) * A_DK].astype(F32)
                q = qkv_ref[0, rows, h * A_DK:(h + 1) * A_DK].astype(F32)
                v = qkv_ref[0, rows, 2 * A_KD + h * A_DV:2 * A_KD + (h + 1) * A_DV].astype(F32)
                gcol = gc[n][:, h:h + 1]
                bcol = gb_ref[0, rows, LANES + h:LANES + h + 1]
                egc = jnp.exp(gcol)
                kbeta = k * bcol
                xs.append(jnp.concatenate([v * bcol, kbeta * egc], axis=1))
                wq_scr[n, h, c:, :] = (q * egc).astype(wq_scr.dtype)
                kd_scr[n, h] = (k * jnp.exp(gc[n][c - 1:c, h:h + 1] - gcol)).astype(kd_scr.dtype)
            sol = _dot(t_mat[n, p].astype(BF16), block_rows(xs))
            for i in range(gp):
                h = p * gp + i
                ub_scr[n, h] = sol[:, i * 2 * A_DV:i * 2 * A_DV + A_DV]
                wq_scr[n, h, :c, :] = sol[:, i * 2 * A_DV + A_DV:(i + 1) * 2 * A_DV].astype(wq_scr.dtype)
        yield

    heads = range(A_HEADS)
    for n in range(nc):
        s = [(s_ref if carry else s_in_ref)[st(n, h)] for h in heads]
        s16 = [s[h].astype(BF16) for h in heads]
        wq = [_dot(wq_scr[n, h].astype(BF16), s16[h]) for h in heads]
        yield
        u = [ub_scr[n, h] - wq[h][:c] for h in heads]
        for p in range(npk):
            hs = [p * gp + i for i in range(gp)]
            o_pk = _dot(qk_scr[n, p].astype(BF16), block_rows([u[h] for h in hs]))
            o_scr[n * c:(n + 1) * c, p * gp * A_DV:(p + 1) * gp * A_DV] = (
                jnp.concatenate([wq[h][c:] for h in hs], axis=1) + o_pk)
        ku = [_dot_tn(kd_scr[n, h].astype(BF16), u[h].astype(BF16)) for h in heads]
        for h in heads:
            s_ref[st(n, h)] = s[h] * jnp.exp(gc[n][c - 1:c, h:h + 1]) + ku[h]
        yield

    for h in heads:
        vs = slice(h * A_DV, (h + 1) * A_DV)
        oh = o_scr[:, vs]
        oh = oh * lax.rsqrt(jnp.mean(oh * oh, axis=-1, keepdims=True) + EPS) * og_ref[...]
        o_ref[0, :, vs] = (oh * zg_ref[0, :, vs].astype(F32)).astype(o_ref.dtype)


def _delta_kernel(qkv_ref, zg_ref, gb_ref, s0_ref, og_ref, o_ref, s_ref, *scratch, tl, c, carry):
    if carry:
        @pl.when(pl.program_id(1) == 0)
        def _():
            s_ref[...] = s0_ref[...]
    _run(_delta_stages(qkv_ref, zg_ref, gb_ref, s0_ref, og_ref, o_ref, s_ref, *scratch,
                       tl=tl, c=c, carry=carry, s1_group=tl // c))


def _delta(qkv, zg, gb, s0, out_g, tl, c, carry):
    b, l, _ = qkv.shape
    assert carry or l == tl
    tok = lambda n: pl.BlockSpec((1, tl, n), lambda i, j: (i, j, 0))
    state = pl.BlockSpec((1 if carry else tl // c, A_HEADS, A_DK, A_DV), lambda i, j: (i, 0, 0, 0))
    return pl.pallas_call(
        functools.partial(_delta_kernel, tl=tl, c=c, carry=carry),
        grid=(b, l // tl),
        in_specs=[tok(A_CONV_CH), tok(A_VD), tok(2 * LANES), state, _const_spec(out_g)],
        out_specs=[tok(A_VD), state],
        out_shape=[jax.ShapeDtypeStruct((b, l, A_VD), BF16), jax.ShapeDtypeStruct(s0.shape, F32)],
        scratch_shapes=_delta_scratch(tl, c),
        compiler_params=_params(2),
        name="delta",
    )(qkv, zg, gb, s0, out_g)


def _mixer_a_kernel(x_ref, cb_ref, s0_ref, ng_ref, wqkv_ref, wz_ref, wab_ref, cw_ref, gp_ref, og_ref,
                    o_ref, s_ref, c_ref, ext_scr, qkv_a, zg_a, gb_a, qkv_b, zg_b, gb_b, *scratch,
                    tl, c, nt, n_tiles):
    k = pl.program_id(0)
    tile_p = jnp.minimum(k, n_tiles - 1)
    tile_d = jnp.maximum(k - 1, 0)

    @pl.when(k == 0)
    def _():
        qkv_b[...] = jnp.zeros_like(qkv_b)
        zg_b[...] = jnp.zeros_like(zg_b)
        gb_b[...] = jnp.zeros_like(gb_b)

    @pl.when(tile_p % nt == 0)
    def _():
        ext_scr[:, SUBLANES - (CONV_W - 1):SUBLANES, :] = cb_ref[...]

    @pl.when(tile_d % nt == 0)
    def _():
        s_ref[...] = s0_ref[...]

    def body(write, read):
        xn = _rms(x_ref[0], ng_ref[...]).astype(BF16)
        proj = _proj_conv_stages(xn, 1, tl, ext_scr, c_ref, wqkv_ref, wz_ref, wab_ref, cw_ref, gp_ref, *write)
        delta = _delta_stages(*read, s0_ref, og_ref, o_ref, s_ref, *scratch, tl=tl, c=c, carry=True,
                              s1_group=MIXER_S1_GROUP)
        _weave(delta, proj, WEAVE_EVERY)

    @pl.when(k % 2 == 0)
    def _():
        body((qkv_a, zg_a, gb_a), (qkv_b, zg_b, gb_b))

    @pl.when(k % 2 == 1)
    def _():
        body((qkv_b, zg_b, gb_b), (qkv_a, zg_a, gb_a))


def _mixer_a(x, cbuf, s0, w, tl, c):
    b, l, _ = x.shape
    nt = l // tl
    n_tiles = b * nt
    p_idx = lambda k: jnp.minimum(k, n_tiles - 1)
    d_idx = lambda k: jnp.maximum(k - 1, 0)
    consts = _proj_consts(w) + [w["out_g"]]
    hand_off = [pltpu.VMEM((1, tl, A_CONV_CH), BF16), pltpu.VMEM((1, tl, A_VD), BF16),
                pltpu.VMEM((1, tl, 2 * LANES), F32)]
    return pl.pallas_call(
        functools.partial(_mixer_a_kernel, tl=tl, c=c, nt=nt, n_tiles=n_tiles),
        grid=(n_tiles + 1,),
        in_specs=[pl.BlockSpec((1, tl, D_MODEL), lambda k: (p_idx(k) // nt, p_idx(k) % nt, 0)),
                  pl.BlockSpec((1, CONV_W - 1, A_CONV_CH), lambda k: (p_idx(k) // nt, 0, 0)),
                  pl.BlockSpec((1, A_HEADS, A_DK, A_DV), lambda k: (d_idx(k) // nt, 0, 0, 0))]
                 + [_const_spec(cst) for cst in consts],
        out_specs=[pl.BlockSpec((1, tl, A_VD), lambda k: (d_idx(k) // nt, d_idx(k) % nt, 0)),
                   pl.BlockSpec((1, A_HEADS, A_DK, A_DV), lambda k: (d_idx(k) // nt, 0, 0, 0)),
                   pl.BlockSpec((1, CONV_W - 1, A_CONV_CH), lambda k: (p_idx(k) // nt, 0, 0))],
        out_shape=[jax.ShapeDtypeStruct((b, l, A_VD), BF16),
                   jax.ShapeDtypeStruct((b, A_HEADS, A_DK, A_DV), F32),
                   jax.ShapeDtypeStruct((b, CONV_W - 1, A_CONV_CH), F32)],
        scratch_shapes=[pltpu.VMEM((1, tl + SUBLANES, A_CONV_CH), F32)] + hand_off + hand_off
                       + _delta_scratch(tl, c),
        compiler_params=_params(1),
        name="mixer_a",
    )(x, cbuf, s0, *consts)


def _post_stages(mix_ref, res_ref, pe_ref, wmix_ref, fg_ref, wgu_ref, wd_ref,
                 pg_ref, wpg_ref, wpp_ref, n_chunks, finish):
    h = res_ref[...] + _dot(mix_ref[...], wmix_ref[...])
    xn = _rms(h, fg_ref[...]).astype(BF16)
    yield
    fc = FFN_HIDDEN // n_chunks
    acc = h
    for i in range(n_chunks):
        gt = _dot(xn, wgu_ref[:, i * fc:(i + 1) * fc])
        up = _dot(xn, wgu_ref[:, FFN_HIDDEN + i * fc:FFN_HIDDEN + (i + 1) * fc])
        yield
        acc = acc + _dot((_silu(gt) * up).astype(BF16), wd_ref[i * fc:(i + 1) * fc, :])
        yield
    h2 = acc
    gate = _sigmoid(_dot(_rms(h2, pg_ref[...]).astype(BF16), wpg_ref[...]))
    yield
    finish(h2 + _dot(pe_ref[...].astype(BF16), wpp_ref[...]) * gate)


def _post_mid_kernel(mix_ref, res_ref, pe_ref, wmix_ref, fg_ref, wgu_ref, wd_ref,
                     pg_ref, wpg_ref, wpp_ref, kvn_ref, kvw_ref, bn_ref, wq_ref,
                     h_ref, q_ref, kv_ref, *, n_chunks):
    def finish(h3):
        h_ref[...] = h3
        kv_ref[...] = _dot(_rms(h3, kvn_ref[...]).astype(BF16), kvw_ref[...])
        q = _dot(_rms(h3, bn_ref[...]).astype(BF16), wq_ref[...]) * (B_HEAD_DIM ** -0.5)
        q_ref[...] = q.astype(q_ref.dtype)

    _run(_post_stages(mix_ref, res_ref, pe_ref, wmix_ref, fg_ref, wgu_ref, wd_ref,
                      pg_ref, wpg_ref, wpp_ref, n_chunks, finish))


def _final_norm(fn_ref, y_ref):
    def finish(h3):
        y_ref[...] = _rms(h3, fn_ref[...])
    return finish


def _post_last_kernel(mix_ref, res_ref, pe_ref, wmix_ref, fg_ref, wgu_ref, wd_ref,
                      pg_ref, wpg_ref, wpp_ref, fn_ref, y_ref, *, n_chunks):
    _run(_post_stages(mix_ref, res_ref, pe_ref, wmix_ref, fg_ref, wgu_ref, wd_ref,
                      pg_ref, wpg_ref, wpp_ref, n_chunks, _final_norm(fn_ref, y_ref)))


def _post(mix, res, pe, layer, common, tail, tm, last, q_dtype=BF16):
    m = res.shape[0]
    row = lambda n: pl.BlockSpec((tm, n), lambda i: (i, 0))
    weights = list(common) + list(tail)
    pe_spec = pl.BlockSpec((None, tm, PLE_DIM), lambda i: (layer, i, 0))
    in_specs = [row(mix.shape[1]), row(D_MODEL), pe_spec] + [_const_spec(a, lyr) for a, lyr in weights]
    if last:
        kern = _post_last_kernel
        out_specs = [row(D_MODEL)]
        out_shape = [jax.ShapeDtypeStruct((m, D_MODEL), F32)]
    else:
        kern = _post_mid_kernel
        out_specs = [row(D_MODEL), row(B_QD), row(2 * B_KVD)]
        out_shape = [jax.ShapeDtypeStruct((m, D_MODEL), F32),
                     jax.ShapeDtypeStruct((m, B_QD), q_dtype),
                     jax.ShapeDtypeStruct((m, 2 * B_KVD), F32)]
    return pl.pallas_call(
        functools.partial(kern, n_chunks=FFN_CHUNKS),
        grid=(m // tm,),
        in_specs=in_specs,
        out_specs=out_specs,
        out_shape=out_shape,
        compiler_params=_params(1),
        name="post_last" if last else "post_mid",
    )(mix, res, pe, *[a for a, _ in weights])


def _attn_block_stages(sink_ref, q_at, kv_own, kv_prev, blk_idx, put):
    rows = B_GROUP * WINDOW
    k2 = jnp.concatenate([kv_own[:, :B_KVD], kv_prev[:, :B_KVD]], axis=0).astype(BF16)
    v2 = jnp.concatenate([kv_own[:, B_KVD:], kv_prev[:, B_KVD:]], axis=0).astype(BF16)
    qi = lax.broadcasted_iota(jnp.int32, (rows, WINDOW), 0) % WINDOW
    ci = lax.broadcasted_iota(jnp.int32, (rows, WINDOW), 1)
    own = ci <= qi
    prev_ok = (ci > qi) & ((blk_idx - 1) * WINDOW + ci >= 0)
    grp = lax.broadcasted_iota(jnp.int32, (rows, 1), 0) // WINDOW
    kv_heads = range(B_KV_HEADS)
    sc = []
    for j in kv_heads:
        qs = jnp.concatenate([q_at(j * B_GROUP + g) for g in range(B_GROUP)], axis=0)
        sc.append(_dot_nt(qs, k2[:, j * B_HEAD_DIM:(j + 1) * B_HEAD_DIM]))
    yield
    pv = []
    for j in kv_heads:
        sink = jnp.full((rows, 1), sink_ref[j * B_GROUP + B_GROUP - 1], F32)
        for g in range(B_GROUP - 2, -1, -1):
            sink = jnp.where(grp == g, sink_ref[j * B_GROUP + g], sink)
        s = jnp.where(own, sc[j][:, :WINDOW], jnp.where(prev_ok, sc[j][:, WINDOW:], NEG_INF))
        mx = jnp.maximum(jnp.max(s, axis=-1, keepdims=True), sink)
        p = jnp.exp(s - mx)
        den = jnp.sum(p, axis=-1, keepdims=True) + jnp.exp(sink - mx)
        p2 = jnp.concatenate([jnp.where(own, p, 0.0), jnp.where(own, 0.0, p)], axis=1).astype(BF16)
        pv.append((p2, den))
    outs = []
    for j in kv_heads:
        o = _dot(pv[j][0], v2[:, j * B_HEAD_DIM:(j + 1) * B_HEAD_DIM]) / pv[j][1]
        outs += [o[g * WINDOW:(g + 1) * WINDOW, :] for g in range(B_GROUP)]
    put(jnp.concatenate(outs, axis=1))
    yield


def _attn_tile_stages(sink_ref, q_ref, kv_ref, kvp_ref, first_blk, att_ref, tm):
    for j in range(tm // WINDOW):
        r = slice(j * WINDOW, (j + 1) * WINDOW)
        kv_prev = kvp_ref[...] if j == 0 else kv_ref[(j - 1) * WINDOW:j * WINDOW, :]

        def put(o, r=r):
            att_ref[r, :] = o.astype(att_ref.dtype)

        yield from _attn_block_stages(
            sink_ref, lambda h, r=r: q_ref[r, h * B_HEAD_DIM:(h + 1) * B_HEAD_DIM],
            kv_ref[r, :], kv_prev, first_blk + j, put)


def _layer_b_kernel(sink_ref, q_ref, kv_ref, kvp_ref, res_ref, pe_ref, wmix_ref, fg_ref, wgu_ref, wd_ref,
                    pg_ref, wpg_ref, wpp_ref, fn_ref, y_ref, att_a, att_b, *, tm, nt, n_tiles):
    k = pl.program_id(0)
    first_blk = (jnp.minimum(k, n_tiles - 1) % nt) * (tm // WINDOW)

    @pl.when(k == 0)
    def _():
        att_b[...] = jnp.zeros_like(att_b)

    def body(write, read):
        attn = _attn_tile_stages(sink_ref, q_ref, kv_ref, kvp_ref, first_blk, write, tm)
        post = _post_stages(read, res_ref, pe_ref, wmix_ref, fg_ref, wgu_ref, wd_ref,
                            pg_ref, wpg_ref, wpp_ref, FFN_CHUNKS, _final_norm(fn_ref, y_ref))
        _weave(attn, post, 1)

    @pl.when(k % 2 == 0)
    def _():
        body(att_a, att_b)

    @pl.when(k % 2 == 1)
    def _():
        body(att_b, att_a)


def _layer_b(q, kv, sinks, res, pe, layer, common, tail, tm, l):
    m = res.shape[0]
    nt = l // tm
    n_tiles = m // tm
    bpt = tm // WINDOW
    p_idx = lambda k: jnp.minimum(k, n_tiles - 1)
    d_idx = lambda k: jnp.maximum(k - 1, 0)
    weights = list(common) + list(tail)
    return pl.pallas_call(
        functools.partial(_layer_b_kernel, tm=tm, nt=nt, n_tiles=n_tiles),
        grid=(n_tiles + 1,),
        in_specs=[pl.BlockSpec(memory_space=pltpu.SMEM),
                  pl.BlockSpec((tm, B_QD), lambda k: (p_idx(k), 0)),
                  pl.BlockSpec((tm, 2 * B_KVD), lambda k: (p_idx(k), 0)),
                  pl.BlockSpec((WINDOW, 2 * B_KVD), lambda k: (jnp.maximum(p_idx(k) * bpt - 1, 0), 0)),
                  pl.BlockSpec((tm, D_MODEL), lambda k: (d_idx(k), 0)),
                  pl.BlockSpec((None, tm, PLE_DIM), lambda k: (layer, d_idx(k), 0))]
                 + [_const_spec(a, lyr) for a, lyr in weights],
        out_specs=pl.BlockSpec((tm, D_MODEL), lambda k: (d_idx(k), 0)),
        out_shape=jax.ShapeDtypeStruct((m, D_MODEL), F32),
        scratch_shapes=[pltpu.VMEM((tm, B_QD), BF16), pltpu.VMEM((tm, B_QD), BF16)],
        compiler_params=_params(1),
        name="layer_b",
    )(sinks, q, kv, kv, res, pe, *[a for a, _ in weights])


def _attn_sample_kernel(sink_ref, q_ref, kv_ref, ck_ref, cv_ref, o_ref, *, bb, t, nbuf):
    rows = B_GROUP * t
    qi = lax.broadcasted_iota(jnp.int32, (rows, nbuf + t), 0) % t
    ki = lax.broadcasted_iota(jnp.int32, (rows, nbuf + t), 1)
    rel = qi + nbuf - ki
    mask = (rel >= 0) & (rel < WINDOW)
    grp = lax.broadcasted_iota(jnp.int32, (rows, 1), 0) // t
    items = [(b, j) for b in range(bb) for j in range(B_KV_HEADS)]
    sink_col = {}
    for j in range(B_KV_HEADS):
        col = jnp.full((rows, 1), sink_ref[j * B_GROUP + B_GROUP - 1], F32)
        for g in range(B_GROUP - 2, -1, -1):
            col = jnp.where(grp == g, sink_ref[j * B_GROUP + g], col)
        sink_col[j] = col
    keys, vals = {}, {}
    for b in range(bb):
        kv = kv_ref[b]
        keys[b] = jnp.concatenate([ck_ref[b], kv[:, :B_KVD]], axis=0).astype(BF16)
        vals[b] = jnp.concatenate([cv_ref[b], kv[:, B_KVD:]], axis=0).astype(BF16)
    s = {}
    for b, j in items:
        qs = jnp.concatenate(
            [q_ref[b, :, (j * B_GROUP + g) * B_HEAD_DIM:(j * B_GROUP + g + 1) * B_HEAD_DIM]
             for g in range(B_GROUP)], axis=0).astype(BF16)
        s[b, j] = jnp.where(mask, _dot_nt(qs, keys[b][:, j * B_HEAD_DIM:(j + 1) * B_HEAD_DIM]), NEG_INF)
    p, den = {}, {}
    for b, j in items:
        mx = jnp.maximum(jnp.max(s[b, j], axis=-1, keepdims=True), sink_col[j])
        p[b, j] = jnp.exp(s[b, j] - mx)
        den[b, j] = jnp.sum(p[b, j], axis=-1, keepdims=True) + jnp.exp(sink_col[j] - mx)
    o = {}
    for b, j in items:
        o[b, j] = _dot(p[b, j].astype(BF16), vals[b][:, j * B_HEAD_DIM:(j + 1) * B_HEAD_DIM]) / den[b, j]
    for b in range(bb):
        o_ref[b] = jnp.concatenate(
            [o[b, j][g * t:(g + 1) * t, :] for j in range(B_KV_HEADS) for g in range(B_GROUP)],
            axis=1).astype(o_ref.dtype)


def _attn_sample(q, kv, ck, cv, sinks, bb):
    b, t, _ = q.shape
    nbuf = ck.shape[1]
    per_b = lambda r, n: pl.BlockSpec((bb, r, n), lambda i: (i, 0, 0))
    return pl.pallas_call(
        functools.partial(_attn_sample_kernel, bb=bb, t=t, nbuf=nbuf),
        grid=(b // bb,),
        in_specs=[pl.BlockSpec(memory_space=pltpu.SMEM),
                  per_b(t, B_QD), per_b(t, 2 * B_KVD), per_b(nbuf, B_KVD), per_b(nbuf, B_KVD)],
        out_specs=per_b(t, B_QD),
        out_shape=jax.ShapeDtypeStruct((b, t, B_QD), BF16),
        compiler_params=_params(1),
        name="attn_sample",
    )(sinks, q, kv, ck, cv)


def _pad_lanes(v):
    return jnp.pad(v, ((0, 0), (0, LANES - v.shape[1])))


def _prepare(p):
    w_in = p["a_w_in"][0]
    n_gate = A_CONV_CH + A_VD
    row = lambda v: v.reshape(1, -1).astype(F32)
    gu, down = p["ffn_w_gu"].astype(BF16), p["ffn_w_down"].astype(BF16)
    ple_gate, ple_proj = p["ple_w_gate"].astype(BF16), p["ple_w_proj"].astype(BF16)

    def post_common(i, w_mix):
        return [(w_mix.astype(BF16), None), (row(p["ffn_norm"][i]), None), (gu, i), (down, i),
                (row(p["ple_norm"][i]), None), (ple_gate, i), (ple_proj, i)]

    return {
        "a_norm": row(p["a_norm"][0]),
        "w_qkv": w_in[:, :A_CONV_CH].astype(BF16),
        "w_z": w_in[:, A_CONV_CH:n_gate].astype(BF16),
        "w_ab": jnp.concatenate([_pad_lanes(w_in[:, n_gate:n_gate + A_HEADS]),
                                 _pad_lanes(w_in[:, n_gate + A_HEADS:])], axis=1).astype(BF16),
        "conv_w": p["a_conv_w"][0],
        "gate_prm": jnp.concatenate([_pad_lanes(row(p["a_a_log"][0])),
                                     _pad_lanes(row(p["a_dt_bias"][0]))], axis=0),
        "out_g": row(p["a_out_norm"][0]),
        "post0": post_common(0, p["a_w_out"][0]),
        "tail0": [(row(p["kv_norm"]), None), (p["kv_w"].astype(BF16), None),
                  (row(p["b_norm"][0]), None), (p["b_w_q"][0].astype(BF16), None)],
        "post1": post_common(1, p["b_w_o"][0]),
        "tail1": [(row(p["final_norm"]), None)],
        "sinks": p["b_sinks"][0].astype(F32),
    }


def _tiles(b, l, prompt):
    m = b * l
    c = min(CHUNK, l)
    if prompt:
        return dict(tm=min(512, m), mx_tl=min(512, l), c=c)
    return dict(tm=min(512, m), pc_bb=min(32, b), dl_tl=min(8 * l, m), c=c, at_bb=min(8, b))


def _trunk(x, pe, s_init, c_init, k_buf, v_buf, w):
    b, l, _ = x.shape
    m = b * l
    prompt = k_buf is None
    ts = _tiles(b, l, prompt)
    act_dtype = BF16 if prompt else F32
    if prompt:
        o, s_new, c_new = _mixer_a(x, c_init, s_init, w, ts["mx_tl"], ts["c"])
    else:
        qkv, zg, gb, c_new = _proj_conv(x, c_init, w, ts["pc_bb"], l, act_dtype)
        grp = lambda a: a.reshape(m // ts["dl_tl"], ts["dl_tl"], a.shape[-1])
        o, s_new = _delta(grp(qkv), grp(zg), grp(gb), s_init, w["out_g"], ts["dl_tl"], ts["c"], False)
    x2d = x.reshape(m, D_MODEL)
    pe2d = pe.reshape(pe.shape[0], m, PLE_DIM)
    h, q, kv = _post(o.reshape(m, -1), x2d, pe2d, 0, w["post0"], w["tail0"], ts["tm"], False, act_dtype)
    kv3 = kv.reshape(b, l, -1)
    heads = lambda a: a.reshape(a.shape[0], a.shape[1], B_KV_HEADS, B_HEAD_DIM)
    if prompt:
        y = _layer_b(q, kv, w["sinks"], h, pe2d, 1, w["post1"], w["tail1"], ts["tm"], l)
        last = kv3[:, l - WINDOW:]
        k_win, v_win = heads(last[:, :, :B_KVD]), heads(last[:, :, B_KVD:])
    else:
        nbuf = k_buf.shape[1]
        att = _attn_sample(q.reshape(b, l, -1), kv3, k_buf.reshape(b, nbuf, B_KVD),
                           v_buf.reshape(b, nbuf, B_KVD), w["sinks"], ts["at_bb"])
        k_win = jnp.concatenate([k_buf, heads(kv3[:, :, :B_KVD])], axis=1)[:, -nbuf:]
        v_win = jnp.concatenate([v_buf, heads(kv3[:, :, B_KVD:])], axis=1)[:, -nbuf:]
        (y,) = _post(att.reshape(m, -1), h, pe2d, 1, w["post1"], w["tail1"], ts["tm"], True)
    return y.reshape(b, l, D_MODEL), s_new[None], c_new[None], k_win, v_win


def kernel(x_prompt, x_sample, state_delta, state_conv, cache_k_win, cache_v_win, p_prompt, p_sample, a_norm, a_w_in, a_conv_w, a_a_log, a_dt_bias, a_out_norm, a_w_out, kv_norm, kv_w, b_norm, b_w_q, b_sinks, b_w_o, ffn_norm, ffn_w_gu, ffn_w_down, ple_norm, ple_w_proj, ple_w_gate, final_norm):
    w = _prepare({
        "a_norm": a_norm, "a_w_in": a_w_in, "a_conv_w": a_conv_w, "a_a_log": a_a_log,
        "a_dt_bias": a_dt_bias, "a_out_norm": a_out_norm, "a_w_out": a_w_out,
        "kv_norm": kv_norm, "kv_w": kv_w, "b_norm": b_norm, "b_w_q": b_w_q, "b_sinks": b_sinks,
        "b_w_o": b_w_o, "ffn_norm": ffn_norm, "ffn_w_gu": ffn_w_gu, "ffn_w_down": ffn_w_down,
        "ple_norm": ple_norm, "ple_w_proj": ple_w_proj, "ple_w_gate": ple_w_gate,
        "final_norm": final_norm,
    })
    bp = x_prompt.shape[0]
    s0 = jnp.zeros((bp, A_HEADS, A_DK, A_DV), F32)
    c0 = jnp.zeros((bp, CONV_W - 1, A_CONV_CH), x_prompt.dtype)
    y_p, sd_p, sc_p, kw_p, vw_p = _trunk(x_prompt, p_prompt, s0, c0, None, None, w)
    y_s, sd_s, sc_s, kw_s, vw_s = _trunk(x_sample, p_sample, state_delta[0], state_conv[0],
                                         cache_k_win, cache_v_win, w)
    return (y_p, y_s, sd_p, sd_s, sc_p, sc_s, kw_p, kw_s, vw_p, vw_s)
```

```python
import functools

import jax
import jax.numpy as jnp
from jax import lax
from jax.experimental import pallas as pl
from jax.experimental.pallas import tpu as pltpu

F32 = jnp.float32
BF16 = jnp.bfloat16

D_MODEL = 1024
A_HEADS = 8
A_DK = 128
A_DV = 128
A_KD = A_HEADS * A_DK
A_VD = A_HEADS * A_DV
A_CONV_CH = 2 * A_KD + A_VD
CONV_W = 4
CHUNK = 64
B_Q_HEADS = 16
B_KV_HEADS = 4
B_HEAD_DIM = 64
B_GROUP = B_Q_HEADS // B_KV_HEADS
B_QD = B_Q_HEADS * B_HEAD_DIM
B_KVD = B_KV_HEADS * B_HEAD_DIM
WINDOW = 128
FFN_HIDDEN = 2816
PLE_DIM = 256
EPS = 1e-6
L2_EPS = 1e-6
NEG_INF = -1e30

LANES = 128
SUBLANES = 8
VMEM_LIMIT_BYTES = 56 * 1024 * 1024
PROJ_SLAB = 512
WEAVE_EVERY = 4
MIXER_S1_GROUP = 4
FFN_CHUNKS = 2


def _rms(x, g):
    return x * lax.rsqrt(jnp.mean(x * x, axis=-1, keepdims=True) + EPS) * g


def _sigmoid(x):
    return 1.0 / (1.0 + jnp.exp(-x))


def _silu(x):
    return x * _sigmoid(x)


def _softplus(x):
    return jnp.maximum(x, 0.0) + jnp.log1p(jnp.exp(-jnp.abs(x)))


def _dot(a, b):
    return jnp.dot(a, b, preferred_element_type=F32)


def _dot_nt(a, b):
    return lax.dot_general(a, b, (((1,), (1,)), ((), ())), preferred_element_type=F32)


def _dot_tn(a, b):
    return lax.dot_general(a, b, (((0,), (0,)), ((), ())), preferred_element_type=F32)


def _split3(x):
    p0 = x.astype(BF16)
    r = x - p0.astype(F32)
    p1 = r.astype(BF16)
    p2 = (r - p1.astype(F32)).astype(BF16)
    return p0, p1, p2


def _const_spec(arr, layer=None):
    if layer is None:
        n = arr.ndim
        return pl.BlockSpec(arr.shape, lambda *_: (0,) * n, pipeline_mode=pl.Buffered(1))
    n = arr.ndim - 1
    return pl.BlockSpec((None,) + arr.shape[1:], lambda *_: (layer,) + (0,) * n,
                        pipeline_mode=pl.Buffered(1))


def _params(n_axes):
    return pltpu.CompilerParams(dimension_semantics=("arbitrary",) * n_axes,
                                vmem_limit_bytes=VMEM_LIMIT_BYTES)


def _run(gen):
    for _ in gen:
        pass


def _weave(main, side, every):
    next(side, None)
    for i, _ in enumerate(main):
        if (i + 1) % every == 0:
            next(side, None)
    _run(side)


def _proj_conv_stages(xn, bb, tl, ext_scr, c_ref, wqkv_ref, wz_ref, wab_ref, cw_ref, gp_ref,
                      qkv_out, z_out, gb_out):
    hist = CONV_W - 1
    pad = SUBLANES
    for s in range(A_CONV_CH // PROJ_SLAB):
        c0 = s * PROJ_SLAB
        cols = slice(c0, c0 + PROJ_SLAB)
        p = _dot(xn, wqkv_ref[:, cols]).reshape(bb, tl, PROJ_SLAB)
        ext_scr[:, pad:pad + tl, cols] = p
        conv = p * cw_ref[hist:hist + 1, cols]
        for j in range(1, CONV_W):
            conv = conv + ext_scr[:, pad - j:pad - j + tl, cols] * cw_ref[hist - j:hist - j + 1, cols]
        tail = p[:, tl - hist:, :]
        ext_scr[:, pad - hist:pad, cols] = tail
        c_ref[:, :, cols] = tail
        act = _silu(conv)
        if c0 < 2 * A_KD:
            scale = A_DK ** -0.5 if c0 < A_KD else 1.0
            for h in range(PROJ_SLAB // A_DK):
                a = act[:, :, h * A_DK:(h + 1) * A_DK]
                a = a * lax.rsqrt(jnp.sum(a * a, axis=-1, keepdims=True) + L2_EPS) * scale
                qkv_out[:, :, c0 + h * A_DK:c0 + (h + 1) * A_DK] = a.astype(qkv_out.dtype)
        else:
            qkv_out[:, :, cols] = act.astype(qkv_out.dtype)
        yield
    for s in range(A_VD // PROJ_SLAB):
        cols = slice(s * PROJ_SLAB, (s + 1) * PROJ_SLAB)
        z = _dot(xn, wz_ref[:, cols])
        z_out[:, :, cols] = z.reshape(bb, tl, PROJ_SLAB).astype(z_out.dtype)
        yield
    ab = _dot(xn, wab_ref[...])
    g = -jnp.exp(gp_ref[0:1, :]) * _softplus(ab[:, :LANES] + gp_ref[1:2, :])
    beta = _sigmoid(ab[:, LANES:])
    gb_out[...] = jnp.concatenate([g, beta], axis=1).reshape(bb, tl, 2 * LANES)
    yield


def _proj_conv_kernel(x_ref, cb_ref, ng_ref, wqkv_ref, wz_ref, wab_ref, cw_ref, gp_ref,
                      qkv_ref, z_ref, gb_ref, c_ref, ext_scr, *, bb, tl):
    @pl.when(pl.program_id(1) == 0)
    def _():
        ext_scr[:, SUBLANES - (CONV_W - 1):SUBLANES, :] = cb_ref[...]

    xn = _rms(x_ref[...].reshape(bb * tl, D_MODEL), ng_ref[...]).astype(BF16)
    _run(_proj_conv_stages(xn, bb, tl, ext_scr, c_ref, wqkv_ref, wz_ref, wab_ref, cw_ref, gp_ref,
                           qkv_ref, z_ref, gb_ref))


def _proj_consts(w):
    return [w["a_norm"], w["w_qkv"], w["w_z"], w["w_ab"], w["conv_w"], w["gate_prm"]]


def _proj_conv(x, cbuf, w, bb, tl, act_dtype):
    b, l, _ = x.shape
    tok = lambda n: pl.BlockSpec((bb, tl, n), lambda i, j: (i, j, 0))
    hist = pl.BlockSpec((bb, CONV_W - 1, A_CONV_CH), lambda i, j: (i, 0, 0))
    consts = _proj_consts(w)
    return pl.pallas_call(
        functools.partial(_proj_conv_kernel, bb=bb, tl=tl),
        grid=(b // bb, l // tl),
        in_specs=[tok(D_MODEL), hist] + [_const_spec(c) for c in consts],
        out_specs=[tok(A_CONV_CH), tok(A_VD), tok(2 * LANES), hist],
        out_shape=[jax.ShapeDtypeStruct((b, l, A_CONV_CH), act_dtype),
                   jax.ShapeDtypeStruct((b, l, A_VD), act_dtype),
                   jax.ShapeDtypeStruct((b, l, 2 * LANES), F32),
                   jax.ShapeDtypeStruct((b, CONV_W - 1, A_CONV_CH), F32)],
        scratch_shapes=[pltpu.VMEM((bb, tl + SUBLANES, A_CONV_CH), F32)],
        compiler_params=_params(2),
        name="proj_conv",
    )(x, cbuf, *consts)


def _delta_scratch(tl, c):
    nc = tl // c
    gp = min(LANES // c, A_HEADS)
    mm_dtype = BF16 if c % 16 == 0 else F32
    return [pltpu.VMEM((nc, A_HEADS, 2 * c, A_DK), mm_dtype),
            pltpu.VMEM((nc, A_HEADS, c, A_DV), F32),
            pltpu.VMEM((nc, A_HEADS, c, A_DK), mm_dtype),
            pltpu.VMEM((nc, A_HEADS // gp, c, gp * c), mm_dtype)]


def _delta_stages(qkv_ref, gb_ref, s_in_ref, o_ref, s_ref,
                  wq_scr, ub_scr, kd_scr, qk_scr, *, tl, c, carry, s1_group):
    gp = min(LANES // c, A_HEADS)
    npk = A_HEADS // gp
    pw = gp * c
    nc = tl // c
    st = (lambda n, h: (0, h)) if carry else (lambda n, h: (n, h))

    ri = lax.broadcasted_iota(jnp.int32, (c, pw), 0)
    li = lax.broadcasted_iota(jnp.int32, (c, pw), 1)
    cj = li % c
    blk = li // c
    incl = ri >= cj
    strict = ri > cj
    eye = (ri == cj).astype(F32)
    tri = (lax.broadcasted_iota(jnp.int32, (c, c), 0) >= lax.broadcasted_iota(jnp.int32, (c, c), 1))
    tri = tri.astype(F32).astype(BF16)
    wide_blk = lax.broadcasted_iota(jnp.int32, (c, gp * A_DK), 1) // A_DK

    def lanes_of(cols, width):
        sel = blk if width == pw else wide_blk
        out = jnp.broadcast_to(cols[-1], (c, width))
        for i in range(gp - 2, -1, -1):
            out = jnp.where(sel == i, jnp.broadcast_to(cols[i], (c, width)), out)
        return out

    def block_diag(y):
        return jnp.concatenate([jnp.where(blk == i, y, 0.0) for i in range(gp)], axis=0).astype(BF16)

    def block_rows(parts):
        w = parts[0].shape[1]
        wide = jnp.concatenate(parts, axis=1)
        sel = lax.broadcasted_iota(jnp.int32, (c, gp * w), 1) // w
        return jnp.concatenate([jnp.where(sel == i, wide, 0.0) for i in range(gp)], axis=0).astype(BF16)

    gc = {}
    for g0 in range(0, nc, s1_group):
        chunks = range(g0, min(g0 + s1_group, nc))
        items = [(n, p) for n in chunks for p in range(npk)]
        for n in chunks:
            ga, gb1, gb2 = _split3(gb_ref[0, n * c:(n + 1) * c, :LANES])
            gc[n] = _dot(tri, ga) + (_dot(tri, gb1) + _dot(tri, gb2))
        gct = {n: gc[n].T for n in chunks}
        decay, a_neg, t_mat = {}, {}, {}
        for n, p in items:
            gcol = lanes_of([gc[n][:, p * gp + i:p * gp + i + 1] for i in range(gp)], pw)
            grow = jnp.concatenate([gct[n][p * gp + i:p * gp + i + 1, :] for i in range(gp)], axis=1)
            decay[n, p] = jnp.where(incl, jnp.exp(jnp.where(incl, gcol - grow, 0.0)), 0.0)
        yield
        for n, p in items:
            rows = slice(n * c, (n + 1) * c)
            kcols = slice(A_KD + p * gp * A_DK, A_KD + (p + 1) * gp * A_DK)
            k_pk = qkv_ref[0, rows, kcols]
            q_pk = qkv_ref[0, rows, p * gp * A_DK:(p + 1) * gp * A_DK]
            beta_w = lanes_of([gb_ref[0, rows, LANES + p * gp + i:LANES + p * gp + i + 1] for i in range(gp)],
                              gp * A_DK)
            k32 = k_pk.astype(F32)
            k_rows = jnp.concatenate([jnp.where(wide_blk == i, k32, 0.0) for i in range(gp)],
                                     axis=0).astype(BF16)
            kk = _dot_nt((k32 * beta_w).astype(BF16), k_rows)
            qk = _dot_nt(q_pk.astype(BF16), k_rows)
            a_neg[n, p] = -jnp.where(strict, kk * decay[n, p], 0.0)
            qk_scr[n, p] = jnp.where(incl, qk * decay[n, p], 0.0).astype(qk_scr.dtype)
        yield
        pw_mat = dict(a_neg)
        for key in items:
            t_mat[key] = eye + a_neg[key]
        m = 2
        while m < c:
            for key in items:
                pw_mat[key] = _dot(pw_mat[key].astype(BF16), block_diag(pw_mat[key]))
            yield
            for key in items:
                t_mat[key] = t_mat[key] + _dot(t_mat[key].astype(BF16), block_diag(pw_mat[key]))
            yield
            m *= 2
        for n, p in items:
            rows = slice(n * c, (n + 1) * c)
            xs = []
            for i in range(gp):
                h = p * gp + i
                k = qkv_ref[0, rows, A_KD + h * A_DK:A_KD + (h + 1) * A_DK].astype(F32)
                q = qkv_ref[0, rows, h * A_DK:(h + 1) * A_DK].astype(F32)
                v = qkv_ref[0, rows, 2 * A_KD + h * A_DV:2 * A_KD + (h + 1) * A_DV].astype(F32)
                gcol = gc[n][:, h:h + 1]
                bcol = gb_ref[0, rows, LANES + h:LANES + h + 1]
                egc = jnp.exp(gcol)
                kbeta = k * bcol
                xs.append(jnp.concatenate([v * bcol, kbeta * egc], axis=1))
                wq_scr[n, h, c:, :] = (q * egc).astype(wq_scr.dtype)
                kd_scr[n, h] = (k * jnp.exp(gc[n][c - 1:c, h:h + 1] - gcol)).astype(kd_scr.dtype)
            sol = _dot(t_mat[n, p].astype(BF16), block_rows(xs))
            for i in range(gp):
                h = p * gp + i
                ub_scr[n, h] = sol[:, i * 2 * A_DV:i * 2 * A_DV + A_DV]
                wq_scr[n, h, :c, :] = sol[:, i * 2 * A_DV + A_DV:(i + 1) * 2 * A_DV].astype(wq_scr.dtype)
        yield

    heads = range(A_HEADS)
    for n in range(nc):
        s = [(s_ref if carry else s_in_ref)[st(n, h)] for h in heads]
        s16 = [s[h].astype(BF16) for h in heads]
        wq = [_dot(wq_scr[n, h].astype(BF16), s16[h]) for h in heads]
        yield
        u = [ub_scr[n, h] - wq[h][:c] for h in heads]
        for p in range(npk):
            hs = [p * gp + i for i in range(gp)]
            o_pk = _dot(qk_scr[n, p].astype(BF16), block_rows([u[h] for h in hs]))
            o_ref[0, n * c:(n + 1) * c, p * gp * A_DV:(p + 1) * gp * A_DV] = (
                jnp.concatenate([wq[h][c:] for h in hs], axis=1) + o_pk)
        ku = [_dot_tn(kd_scr[n, h].astype(BF16), u[h].astype(BF16)) for h in heads]
        for h in heads:
            s_ref[st(n, h)] = s[h] * jnp.exp(gc[n][c - 1:c, h:h + 1]) + ku[h]
        yield


def _delta_kernel(qkv_ref, gb_ref, s0_ref, o_ref, s_ref, *scratch, tl, c, carry):
    if carry:
        @pl.when(pl.program_id(1) == 0)
        def _():
            s_ref[...] = s0_ref[...]
    _run(_delta_stages(qkv_ref, gb_ref, s0_ref, o_ref, s_ref, *scratch,
                       tl=tl, c=c, carry=carry, s1_group=tl // c))


def _delta(qkv, gb, s0, tl, c, carry):
    b, l, _ = qkv.shape
    assert carry or l == tl
    tok = lambda n: pl.BlockSpec((1, tl, n), lambda i, j: (i, j, 0))
    state = pl.BlockSpec((1 if carry else tl // c, A_HEADS, A_DK, A_DV), lambda i, j: (i, 0, 0, 0))
    return pl.pallas_call(
        functools.partial(_delta_kernel, tl=tl, c=c, carry=carry),
        grid=(b, l // tl),
        in_specs=[tok(A_CONV_CH), tok(2 * LANES), state],
        out_specs=[tok(A_VD), state],
        out_shape=[jax.ShapeDtypeStruct((b, l, A_VD), F32), jax.ShapeDtypeStruct(s0.shape, F32)],
        scratch_shapes=_delta_scratch(tl, c),
        compiler_params=_params(2),
        name="delta",
    )(qkv, gb, s0)


def _mixer_a_kernel(x_ref, cb_ref, s0_ref, ng_ref, wqkv_ref, wz_ref, wab_ref, cw_ref, gp_ref,
                    o_ref, z_ref, s_ref, c_ref, ext_scr, qkv_a, gb_a, qkv_b, gb_b, *scratch,
                    tl, c, nt, n_tiles):
    k = pl.program_id(0)
    tile_p = jnp.minimum(k, n_tiles - 1)
    tile_d = jnp.maximum(k - 1, 0)

    @pl.when(k == 0)
    def _():
        qkv_b[...] = jnp.zeros_like(qkv_b)
        gb_b[...] = jnp.zeros_like(gb_b)

    @pl.when(tile_p % nt == 0)
    def _():
        ext_scr[:, SUBLANES - (CONV_W - 1):SUBLANES, :] = cb_ref[...]

    @pl.when(tile_d % nt == 0)
    def _():
        s_ref[...] = s0_ref[...]

    def body(write, read):
        xn = _rms(x_ref[0], ng_ref[...]).astype(BF16)
        proj = _proj_conv_stages(xn, 1, tl, ext_scr, c_ref, wqkv_ref, wz_ref, wab_ref, cw_ref, gp_ref,
                                 write[0], z_ref, write[1])
        delta = _delta_stages(*read, s0_ref, o_ref, s_ref, *scratch, tl=tl, c=c, carry=True,
                              s1_group=MIXER_S1_GROUP)
        _weave(delta, proj, WEAVE_EVERY)

    @pl.when(k % 2 == 0)
    def _():
        body((qkv_a, gb_a), (qkv_b, gb_b))

    @pl.when(k % 2 == 1)
    def _():
        body((qkv_b, gb_b), (qkv_a, gb_a))


def _mixer_a(x, cbuf, s0, w, tl, c):
    b, l, _ = x.shape
    nt = l // tl
    n_tiles = b * nt
    p_idx = lambda k: jnp.minimum(k, n_tiles - 1)
    d_idx = lambda k: jnp.maximum(k - 1, 0)
    consts = _proj_consts(w)
    hand_off = [pltpu.VMEM((1, tl, A_CONV_CH), BF16), pltpu.VMEM((1, tl, 2 * LANES), F32)]
    return pl.pallas_call(
        functools.partial(_mixer_a_kernel, tl=tl, c=c, nt=nt, n_tiles=n_tiles),
        grid=(n_tiles + 1,),
        in_specs=[pl.BlockSpec((1, tl, D_MODEL), lambda k: (p_idx(k) // nt, p_idx(k) % nt, 0)),
                  pl.BlockSpec((1, CONV_W - 1, A_CONV_CH), lambda k: (p_idx(k) // nt, 0, 0)),
                  pl.BlockSpec((1, A_HEADS, A_DK, A_DV), lambda k: (d_idx(k) // nt, 0, 0, 0))]
                 + [_const_spec(cst) for cst in consts],
        out_specs=[pl.BlockSpec((1, tl, A_VD), lambda k: (d_idx(k) // nt, d_idx(k) % nt, 0)),
                   pl.BlockSpec((1, tl, A_VD), lambda k: (p_idx(k) // nt, p_idx(k) % nt, 0)),
                   pl.BlockSpec((1, A_HEADS, A_DK, A_DV), lambda k: (d_idx(k) // nt, 0, 0, 0)),
                   pl.BlockSpec((1, CONV_W - 1, A_CONV_CH), lambda k: (p_idx(k) // nt, 0, 0))],
        out_shape=[jax.ShapeDtypeStruct((b, l, A_VD), F32),
                   jax.ShapeDtypeStruct((b, l, A_VD), BF16),
                   jax.ShapeDtypeStruct((b, A_HEADS, A_DK, A_DV), F32),
                   jax.ShapeDtypeStruct((b, CONV_W - 1, A_CONV_CH), F32)],
        scratch_shapes=[pltpu.VMEM((1, tl + SUBLANES, A_CONV_CH), F32)] + hand_off + hand_off
                       + _delta_scratch(tl, c),
        compiler_params=_params(1),
        name="mixer_a",
    )(x, cbuf, s0, *consts)


def _post_stages(get_mix, res_ref, pe_ref, wmix_ref, fg_ref, wgu_ref, wd_ref,
                 pg_ref, wpg_ref, wpp_ref, n_chunks, finish):
    h = res_ref[...] + _dot(get_mix(), wmix_ref[...])
    xn = _rms(h, fg_ref[...]).astype(BF16)
    yield
    fc = FFN_HIDDEN // n_chunks
    acc = h
    for i in range(n_chunks):
        gt = _dot(xn, wgu_ref[:, i * fc:(i + 1) * fc])
        up = _dot(xn, wgu_ref[:, FFN_HIDDEN + i * fc:FFN_HIDDEN + (i + 1) * fc])
        yield
        acc = acc + _dot((_silu(gt) * up).astype(BF16), wd_ref[i * fc:(i + 1) * fc, :])
        yield
    h2 = acc
    gate = _sigmoid(_dot(_rms(h2, pg_ref[...]).astype(BF16), wpg_ref[...]))
    yield
    finish(h2 + _dot(pe_ref[...].astype(BF16), wpp_ref[...]) * gate)


def _post_mid_kernel(o_nxt_ref, z_nxt_ref, o_first_ref, z_first_ref, res_ref, pe_ref, og_ref,
                     wmix_ref, fg_ref, wgu_ref, wd_ref, pg_ref, wpg_ref, wpp_ref,
                     kvn_ref, kvw_ref, bn_ref, wq_ref, h_ref, q_ref, kv_ref, mix_scr, *, n_chunks):
    def gate_into_scratch(o_ref, z_ref):
        for hd in range(A_HEADS):
            vs = slice(hd * A_DV, (hd + 1) * A_DV)
            oh = o_ref[:, vs]
            oh = oh * lax.rsqrt(jnp.mean(oh * oh, axis=-1, keepdims=True) + EPS) * og_ref[...]
            mix_scr[:, vs] = (oh * _silu(z_ref[:, vs].astype(F32))).astype(BF16)

    @pl.when(pl.program_id(0) == 0)
    def _():
        gate_into_scratch(o_first_ref, z_first_ref)

    def finish(h3):
        h_ref[...] = h3
        kv_ref[...] = _dot(_rms(h3, kvn_ref[...]).astype(BF16), kvw_ref[...])
        q = _dot(_rms(h3, bn_ref[...]).astype(BF16), wq_ref[...]) * (B_HEAD_DIM ** -0.5)
        q_ref[...] = q.astype(q_ref.dtype)

    stages = _post_stages(lambda: mix_scr[...], res_ref, pe_ref, wmix_ref, fg_ref, wgu_ref, wd_ref,
                          pg_ref, wpg_ref, wpp_ref, n_chunks, finish)
    next(stages)
    gate_into_scratch(o_nxt_ref, z_nxt_ref)
    _run(stages)


def _final_norm(fn_ref, y_ref):
    def finish(h3):
        y_ref[...] = _rms(h3, fn_ref[...])
    return finish


def _post_last_kernel(mix_ref, res_ref, pe_ref, wmix_ref, fg_ref, wgu_ref, wd_ref,
                      pg_ref, wpg_ref, wpp_ref, fn_ref, y_ref, *, n_chunks):
    _run(_post_stages(lambda: mix_ref[...], res_ref, pe_ref, wmix_ref, fg_ref, wgu_ref, wd_ref,
                      pg_ref, wpg_ref, wpp_ref, n_chunks, _final_norm(fn_ref, y_ref)))


def _post(mix, res, pe, layer, common, tail, tm, last, q_dtype=BF16):
    m = res.shape[0]
    row = lambda n: pl.BlockSpec((tm, n), lambda i: (i, 0))
    weights = list(common) + list(tail)
    pe_spec = pl.BlockSpec((None, tm, PLE_DIM), lambda i: (layer, i, 0))
    if last:
        rows_in, gain, scratch = [mix], [], []
        mix_specs = [row(mix.shape[1])]
        kern = _post_last_kernel
        out_specs = [row(D_MODEL)]
        out_shape = [jax.ShapeDtypeStruct((m, D_MODEL), F32)]
    else:
        o_raw, z, out_g = mix
        rows_in, gain = [o_raw, z, o_raw, z], [out_g]
        nxt = lambda n: pl.BlockSpec((tm, n), lambda i: (jnp.minimum(i + 1, m // tm - 1), 0))
        first = lambda n: pl.BlockSpec((tm, n), lambda i: (0, 0), pipeline_mode=pl.Buffered(1))
        mix_specs = [nxt(A_VD), nxt(A_VD), first(A_VD), first(A_VD)]
        scratch = [pltpu.VMEM((tm, A_VD), BF16)]
        kern = _post_mid_kernel
        out_specs = [row(D_MODEL), row(B_QD), row(2 * B_KVD)]
        out_shape = [jax.ShapeDtypeStruct((m, D_MODEL), F32),
                     jax.ShapeDtypeStruct((m, B_QD), q_dtype),
                     jax.ShapeDtypeStruct((m, 2 * B_KVD), F32)]
    in_specs = (mix_specs + [row(D_MODEL), pe_spec] + [_const_spec(a) for a in gain]
                + [_const_spec(a, lyr) for a, lyr in weights])
    return pl.pallas_call(
        functools.partial(kern, n_chunks=FFN_CHUNKS),
        grid=(m // tm,),
        in_specs=in_specs,
        out_specs=out_specs,
        out_shape=out_shape,
        scratch_shapes=scratch,
        compiler_params=_params(1),
        name="post_last" if last else "post_mid",
    )(*rows_in, res, pe, *gain, *[a for a, _ in weights])


def _attn_block_stages(sink_ref, q_at, kv_own, kv_prev, blk_idx, put):
    rows = B_GROUP * WINDOW
    k2 = jnp.concatenate([kv_own[:, :B_KVD], kv_prev[:, :B_KVD]], axis=0).astype(BF16)
    v2 = jnp.concatenate([kv_own[:, B_KVD:], kv_prev[:, B_KVD:]], axis=0).astype(BF16)
    qi = lax.broadcasted_iota(jnp.int32, (rows, WINDOW), 0) % WINDOW
    ci = lax.broadcasted_iota(jnp.int32, (rows, WINDOW), 1)
    own = ci <= qi
    prev_ok = (ci > qi) & ((blk_idx - 1) * WINDOW + ci >= 0)
    grp = lax.broadcasted_iota(jnp.int32, (rows, 1), 0) // WINDOW
    kv_heads = range(B_KV_HEADS)
    sc = []
    for j in kv_heads:
        qs = jnp.concatenate([q_at(j * B_GROUP + g) for g in range(B_GROUP)], axis=0)
        sc.append(_dot_nt(qs, k2[:, j * B_HEAD_DIM:(j + 1) * B_HEAD_DIM]))
    yield
    pv = []
    for j in kv_heads:
        sink = jnp.full((rows, 1), sink_ref[j * B_GROUP + B_GROUP - 1], F32)
        for g in range(B_GROUP - 2, -1, -1):
            sink = jnp.where(grp == g, sink_ref[j * B_GROUP + g], sink)
        s = jnp.where(own, sc[j][:, :WINDOW], jnp.where(prev_ok, sc[j][:, WINDOW:], NEG_INF))
        mx = jnp.maximum(jnp.max(s, axis=-1, keepdims=True), sink)
        p = jnp.exp(s - mx)
        den = jnp.sum(p, axis=-1, keepdims=True) + jnp.exp(sink - mx)
        p2 = jnp.concatenate([jnp.where(own, p, 0.0), jnp.where(own, 0.0, p)], axis=1).astype(BF16)
        pv.append((p2, den))
    outs = []
    for j in kv_heads:
        o = _dot(pv[j][0], v2[:, j * B_HEAD_DIM:(j + 1) * B_HEAD_DIM]) / pv[j][1]
        outs += [o[g * WINDOW:(g + 1) * WINDOW, :] for g in range(B_GROUP)]
    put(jnp.concatenate(outs, axis=1))
    yield


def _attn_tile_stages(sink_ref, q_ref, kv_ref, kvp_ref, first_blk, att_ref, tm):
    for j in range(tm // WINDOW):
        r = slice(j * WINDOW, (j + 1) * WINDOW)
        kv_prev = kvp_ref[...] if j == 0 else kv_ref[(j - 1) * WINDOW:j * WINDOW, :]

        def put(o, r=r):
            att_ref[r, :] = o.astype(att_ref.dtype)

        yield from _attn_block_stages(
            sink_ref, lambda h, r=r: q_ref[r, h * B_HEAD_DIM:(h + 1) * B_HEAD_DIM],
            kv_ref[r, :], kv_prev, first_blk + j, put)


def _layer_b_kernel(sink_ref, q_ref, kv_ref, kvp_ref, res_ref, pe_ref, wmix_ref, fg_ref, wgu_ref, wd_ref,
                    pg_ref, wpg_ref, wpp_ref, fn_ref, y_ref, att_a, att_b, *, tm, nt, n_tiles):
    k = pl.program_id(0)
    first_blk = (jnp.minimum(k, n_tiles - 1) % nt) * (tm // WINDOW)

    @pl.when(k == 0)
    def _():
        att_b[...] = jnp.zeros_like(att_b)

    def body(write, read):
        attn = _attn_tile_stages(sink_ref, q_ref, kv_ref, kvp_ref, first_blk, write, tm)
        post = _post_stages(lambda: read[...], res_ref, pe_ref, wmix_ref, fg_ref, wgu_ref, wd_ref,
                            pg_ref, wpg_ref, wpp_ref, FFN_CHUNKS, _final_norm(fn_ref, y_ref))
        _weave(attn, post, 1)

    @pl.when(k % 2 == 0)
    def _():
        body(att_a, att_b)

    @pl.when(k % 2 == 1)
    def _():
        body(att_b, att_a)


def _layer_b(q, kv, sinks, res, pe, layer, common, tail, tm, l):
    m = res.shape[0]
    nt = l // tm
    n_tiles = m // tm
    bpt = tm // WINDOW
    p_idx = lambda k: jnp.minimum(k, n_tiles - 1)
    d_idx = lambda k: jnp.maximum(k - 1, 0)
    weights = list(common) + list(tail)
    return pl.pallas_call(
        functools.partial(_layer_b_kernel, tm=tm, nt=nt, n_tiles=n_tiles),
        grid=(n_tiles + 1,),
        in_specs=[pl.BlockSpec(memory_space=pltpu.SMEM),
                  pl.BlockSpec((tm, B_QD), lambda k: (p_idx(k), 0)),
                  pl.BlockSpec((tm, 2 * B_KVD), lambda k: (p_idx(k), 0)),
                  pl.BlockSpec((WINDOW, 2 * B_KVD), lambda k: (jnp.maximum(p_idx(k) * bpt - 1, 0), 0)),
                  pl.BlockSpec((tm, D_MODEL), lambda k: (d_idx(k), 0)),
                  pl.BlockSpec((None, tm, PLE_DIM), lambda k: (layer, d_idx(k), 0))]
                 + [_const_spec(a, lyr) for a, lyr in weights],
        out_specs=pl.BlockSpec((tm, D_MODEL), lambda k: (d_idx(k), 0)),
        out_shape=jax.ShapeDtypeStruct((m, D_MODEL), F32),
        scratch_shapes=[pltpu.VMEM((tm, B_QD), BF16), pltpu.VMEM((tm, B_QD), BF16)],
        compiler_params=_params(1),
        name="layer_b",
    )(sinks, q, kv, kv, res, pe, *[a for a, _ in weights])


def _attn_sample_kernel(sink_ref, q_ref, kv_ref, ck_ref, cv_ref, o_ref, *, bb, t, nbuf):
    rows = B_GROUP * t
    qi = lax.broadcasted_iota(jnp.int32, (rows, nbuf + t), 0) % t
    ki = lax.broadcasted_iota(jnp.int32, (rows, nbuf + t), 1)
    rel = qi + nbuf - ki
    mask = (rel >= 0) & (rel < WINDOW)
    grp = lax.broadcasted_iota(jnp.int32, (rows, 1), 0) // t
    items = [(b, j) for b in range(bb) for j in range(B_KV_HEADS)]
    sink_col = {}
    for j in range(B_KV_HEADS):
        col = jnp.full((rows, 1), sink_ref[j * B_GROUP + B_GROUP - 1], F32)
        for g in range(B_GROUP - 2, -1, -1):
            col = jnp.where(grp == g, sink_ref[j * B_GROUP + g], col)
        sink_col[j] = col
    keys, vals = {}, {}
    for b in range(bb):
        kv = kv_ref[b]
        keys[b] = jnp.concatenate([ck_ref[b], kv[:, :B_KVD]], axis=0).astype(BF16)
        vals[b] = jnp.concatenate([cv_ref[b], kv[:, B_KVD:]], axis=0).astype(BF16)
    s = {}
    for b, j in items:
        qs = jnp.concatenate(
            [q_ref[b, :, (j * B_GROUP + g) * B_HEAD_DIM:(j * B_GROUP + g + 1) * B_HEAD_DIM]
             for g in range(B_GROUP)], axis=0).astype(BF16)
        s[b, j] = jnp.where(mask, _dot_nt(qs, keys[b][:, j * B_HEAD_DIM:(j + 1) * B_HEAD_DIM]), NEG_INF)
    p, den = {}, {}
    for b, j in items:
        mx = jnp.maximum(jnp.max(s[b, j], axis=-1, keepdims=True), sink_col[j])
        p[b, j] = jnp.exp(s[b, j] - mx)
        den[b, j] = jnp.sum(p[b, j], axis=-1, keepdims=True) + jnp.exp(sink_col[j] - mx)
    o = {}
    for b, j in items:
        o[b, j] = _dot(p[b, j].astype(BF16), vals[b][:, j * B_HEAD_DIM:(j + 1) * B_HEAD_DIM]) / den[b, j]
    for b in range(bb):
        o_ref[b] = jnp.concatenate(
            [o[b, j][g * t:(g + 1) * t, :] for j in range(B_KV_HEADS) for g in range(B_GROUP)],
            axis=1).astype(o_ref.dtype)


def _attn_sample(q, kv, ck, cv, sinks, bb):
    b, t, _ = q.shape
    nbuf = ck.shape[1]
    per_b = lambda r, n: pl.BlockSpec((bb, r, n), lambda i: (i, 0, 0))
    return pl.pallas_call(
        functools.partial(_attn_sample_kernel, bb=bb, t=t, nbuf=nbuf),
        grid=(b // bb,),
        in_specs=[pl.BlockSpec(memory_space=pltpu.SMEM),
                  per_b(t, B_QD), per_b(t, 2 * B_KVD), per_b(nbuf, B_KVD), per_b(nbuf, B_KVD)],
        out_specs=per_b(t, B_QD),
        out_shape=jax.ShapeDtypeStruct((b, t, B_QD), BF16),
        compiler_params=_params(1),
        name="attn_sample",
    )(sinks, q, kv, ck, cv)


def _pad_lanes(v):
    return jnp.pad(v, ((0, 0), (0, LANES - v.shape[1])))


def _prepare(p):
    w_in = p["a_w_in"][0]
    n_gate = A_CONV_CH + A_VD
    row = lambda v: v.reshape(1, -1).astype(F32)
    gu, down = p["ffn_w_gu"].astype(BF16), p["ffn_w_down"].astype(BF16)
    ple_gate, ple_proj = p["ple_w_gate"].astype(BF16), p["ple_w_proj"].astype(BF16)

    def post_common(i, w_mix):
        return [(w_mix.astype(BF16), None), (row(p["ffn_norm"][i]), None), (gu, i), (down, i),
                (row(p["ple_norm"][i]), None), (ple_gate, i), (ple_proj, i)]

    return {
        "a_norm": row(p["a_norm"][0]),
        "w_qkv": w_in[:, :A_CONV_CH].astype(BF16),
        "w_z": w_in[:, A_CONV_CH:n_gate].astype(BF16),
        "w_ab": jnp.concatenate([_pad_lanes(w_in[:, n_gate:n_gate + A_HEADS]),
                                 _pad_lanes(w_in[:, n_gate + A_HEADS:])], axis=1).astype(BF16),
        "conv_w": p["a_conv_w"][0],
        "gate_prm": jnp.concatenate([_pad_lanes(row(p["a_a_log"][0])),
                                     _pad_lanes(row(p["a_dt_bias"][0]))], axis=0),
        "out_g": row(p["a_out_norm"][0]),
        "post0": post_common(0, p["a_w_out"][0]),
        "tail0": [(row(p["kv_norm"]), None), (p["kv_w"].astype(BF16), None),
                  (row(p["b_norm"][0]), None), (p["b_w_q"][0].astype(BF16), None)],
        "post1": post_common(1, p["b_w_o"][0]),
        "tail1": [(row(p["final_norm"]), None)],
        "sinks": p["b_sinks"][0].astype(F32),
    }


def _tiles(b, l, prompt):
    m = b * l
    c = min(CHUNK, l)
    if prompt:
        return dict(tm=min(512, m), mx_tl=min(512, l), c=c)
    return dict(tm=min(256, m), pc_bb=min(32, b), dl_tl=min(8 * l, m), c=c, at_bb=min(8, b))


def _trunk(x, pe, s_init, c_init, k_buf, v_buf, w):
    b, l, _ = x.shape
    m = b * l
    prompt = k_buf is None
    ts = _tiles(b, l, prompt)
    act_dtype = BF16 if prompt else F32
    if prompt:
        o, z, s_new, c_new = _mixer_a(x, c_init, s_init, w, ts["mx_tl"], ts["c"])
    else:
        qkv, z, gb, c_new = _proj_conv(x, c_init, w, ts["pc_bb"], l, act_dtype)
        grp = lambda a: a.reshape(m // ts["dl_tl"], ts["dl_tl"], a.shape[-1])
        o, s_new = _delta(grp(qkv), grp(gb), s_init, ts["dl_tl"], ts["c"], False)
    x2d = x.reshape(m, D_MODEL)
    pe2d = pe.reshape(pe.shape[0], m, PLE_DIM)
    h, q, kv = _post((o.reshape(m, -1), z.reshape(m, -1), w["out_g"]), x2d, pe2d, 0,
                     w["post0"], w["tail0"], ts["tm"], False, act_dtype)
    kv3 = kv.reshape(b, l, -1)
    heads = lambda a: a.reshape(a.shape[0], a.shape[1], B_KV_HEADS, B_HEAD_DIM)
    if prompt:
        y = _layer_b(q, kv, w["sinks"], h, pe2d, 1, w["post1"], w["tail1"], ts["tm"], l)
        last = kv3[:, l - WINDOW:]
        k_win, v_win = heads(last[:, :, :B_KVD]), heads(last[:, :, B_KVD:])
    else:
        nbuf = k_buf.shape[1]
        att = _attn_sample(q.reshape(b, l, -1), kv3, k_buf.reshape(b, nbuf, B_KVD),
                           v_buf.reshape(b, nbuf, B_KVD), w["sinks"], ts["at_bb"])
        k_win = jnp.concatenate([k_buf, heads(kv3[:, :, :B_KVD])], axis=1)[:, -nbuf:]
        v_win = jnp.concatenate([v_buf, heads(kv3[:, :, B_KVD:])], axis=1)[:, -nbuf:]
        (y,) = _post(att.reshape(m, -1), h, pe2d, 1, w["post1"], w["tail1"], ts["tm"], True)
    return y.reshape(b, l, D_MODEL), s_new[None], c_new[None], k_win, v_win


def kernel(x_prompt, x_sample, state_delta, state_conv, cache_k_win, cache_v_win, p_prompt, p_sample, a_norm, a_w_in, a_conv_w, a_a_log, a_dt_bias, a_out_norm, a_w_out, kv_norm, kv_w, b_norm, b_w_q, b_sinks, b_w_o, ffn_norm, ffn_w_gu, ffn_w_down, ple_norm, ple_w_proj, ple_w_gate, final_norm):
    w = _prepare({
        "a_norm": a_norm, "a_w_in": a_w_in, "a_conv_w": a_conv_w, "a_a_log": a_a_log,
        "a_dt_bias": a_dt_bias, "a_out_norm": a_out_norm, "a_w_out": a_w_out,
        "kv_norm": kv_norm, "kv_w": kv_w, "b_norm": b_norm, "b_w_q": b_w_q, "b_sinks": b_sinks,
        "b_w_o": b_w_o, "ffn_norm": ffn_norm, "ffn_w_gu": ffn_w_gu, "ffn_w_down": ffn_w_down,
        "ple_norm": ple_norm, "ple_w_proj": ple_w_proj, "ple_w_gate": ple_w_gate,
        "final_norm": final_norm,
    })
    bp = x_prompt.shape[0]
    s0 = jnp.zeros((bp, A_HEADS, A_DK, A_DV), F32)
    c0 = jnp.zeros((bp, CONV_W - 1, A_CONV_CH), x_prompt.dtype)
    y_p, sd_p, sc_p, kw_p, vw_p = _trunk(x_prompt, p_prompt, s0, c0, None, None, w)
    y_s, sd_s, sc_s, kw_s, vw_s = _trunk(x_sample, p_sample, state_delta[0], state_conv[0],
                                         cache_k_win, cache_v_win, w)
    return (y_p, y_s, sd_p, sd_s, sc_p, sc_s, kw_p, kw_s, vw_p, vw_s)
```

```python
import functools

import jax
import jax.numpy as jnp
from jax import lax
from jax.experimental import pallas as pl
from jax.experimental.pallas import tpu as pltpu

F32 = jnp.float32
BF16 = jnp.bfloat16

D_MODEL = 1024
A_HEADS = 8
A_DK = 128
A_DV = 128
A_KD = A_HEADS * A_DK
A_VD = A_HEADS * A_DV
A_CONV_CH = 2 * A_KD + A_VD
CONV_W = 4
CHUNK = 64
B_Q_HEADS = 16
B_KV_HEADS = 4
B_HEAD_DIM = 64
B_GROUP = B_Q_HEADS // B_KV_HEADS
B_QD = B_Q_HEADS * B_HEAD_DIM
B_KVD = B_KV_HEADS * B_HEAD_DIM
WINDOW = 128
FFN_HIDDEN = 2816
PLE_DIM = 256
EPS = 1e-6
L2_EPS = 1e-6
NEG_INF = -1e30

LANES = 128
SUBLANES = 8
VMEM_LIMIT_BYTES = 56 * 1024 * 1024
PROJ_SLAB = 512
WEAVE_EVERY = 4
MIXER_S1_GROUP = 4
FFN_CHUNKS = 2


def _rms(x, g):
    return x * lax.rsqrt(jnp.mean(x * x, axis=-1, keepdims=True) + EPS) * g


def _sigmoid(x):
    return 1.0 / (1.0 + jnp.exp(-x))


def _silu(x):
    return x * _sigmoid(x)


def _softplus(x):
    return jnp.maximum(x, 0.0) + jnp.log1p(jnp.exp(-jnp.abs(x)))


def _dot(a, b):
    return jnp.dot(a, b, preferred_element_type=F32)


def _dot_nt(a, b):
    return lax.dot_general(a, b, (((1,), (1,)), ((), ())), preferred_element_type=F32)


def _dot_tn(a, b):
    return lax.dot_general(a, b, (((0,), (0,)), ((), ())), preferred_element_type=F32)


def _split3(x):
    p0 = x.astype(BF16)
    r = x - p0.astype(F32)
    p1 = r.astype(BF16)
    p2 = (r - p1.astype(F32)).astype(BF16)
    return p0, p1, p2


def _const_spec(arr, layer=None):
    if layer is None:
        n = arr.ndim
        return pl.BlockSpec(arr.shape, lambda *_: (0,) * n, pipeline_mode=pl.Buffered(1))
    n = arr.ndim - 1
    return pl.BlockSpec((None,) + arr.shape[1:], lambda *_: (layer,) + (0,) * n,
                        pipeline_mode=pl.Buffered(1))


def _params(n_axes):
    return pltpu.CompilerParams(dimension_semantics=("arbitrary",) * n_axes,
                                vmem_limit_bytes=VMEM_LIMIT_BYTES)


def _run(gen):
    for _ in gen:
        pass


def _weave(main, side, every):
    next(side, None)
    for i, _ in enumerate(main):
        if (i + 1) % every == 0:
            next(side, None)
    _run(side)


def _proj_conv_stages(xn, bb, tl, hist_scr, c_ref, wqkv_ref, wz_ref, wab_ref, cw_ref, gp_ref,
                      qkv_out, z_out, gb_out):
    hist = CONV_W - 1
    sub = SUBLANES
    g = tl // sub
    row_in_group = lax.broadcasted_iota(jnp.int32, (1, 1, sub, 1), 2)
    for s in range(A_CONV_CH // PROJ_SLAB):
        c0 = s * PROJ_SLAB
        cols = slice(c0, c0 + PROJ_SLAB)
        p = _dot(xn, wqkv_ref[:, cols]).reshape(bb, tl, PROJ_SLAB)
        groups = jnp.concatenate([hist_scr[:, :, cols][:, None], p.reshape(bb, g, sub, PROJ_SLAB)], axis=1)
        conv = p * cw_ref[hist:hist + 1, cols]
        for j in range(1, CONV_W):
            rot = pltpu.roll(groups.reshape(bb * (g + 1), sub, PROJ_SLAB), j, axis=1)
            rot = rot.reshape(bb, g + 1, sub, PROJ_SLAB)
            shifted = jnp.where(row_in_group < j, rot[:, :-1], rot[:, 1:])
            conv = conv + shifted.reshape(bb, tl, PROJ_SLAB) * cw_ref[hist - j:hist - j + 1, cols]
        hist_scr[:, :, cols] = p[:, tl - sub:, :]
        tail = p[:, tl - hist:, :]
        c_ref[:, :, cols] = tail
        act = _silu(conv)
        if c0 < 2 * A_KD:
            scale = A_DK ** -0.5 if c0 < A_KD else 1.0
            for h in range(PROJ_SLAB // A_DK):
                a = act[:, :, h * A_DK:(h + 1) * A_DK]
                a = a * lax.rsqrt(jnp.sum(a * a, axis=-1, keepdims=True) + L2_EPS) * scale
                qkv_out[:, :, c0 + h * A_DK:c0 + (h + 1) * A_DK] = a.astype(qkv_out.dtype)
        else:
            qkv_out[:, :, cols] = act.astype(qkv_out.dtype)
        yield
    for s in range(A_VD // PROJ_SLAB):
        cols = slice(s * PROJ_SLAB, (s + 1) * PROJ_SLAB)
        z = _dot(xn, wz_ref[:, cols])
        z_out[:, :, cols] = z.reshape(bb, tl, PROJ_SLAB).astype(z_out.dtype)
        yield
    ab = _dot(xn, wab_ref[...])
    g = -jnp.exp(gp_ref[0:1, :]) * _softplus(ab[:, :LANES] + gp_ref[1:2, :])
    beta = _sigmoid(ab[:, LANES:])
    gb_out[...] = jnp.concatenate([g, beta], axis=1).reshape(bb, tl, 2 * LANES)
    yield


def _init_history(hist_scr, cb_ref):
    cb = cb_ref[...]
    zeros = jnp.zeros((cb.shape[0], SUBLANES - (CONV_W - 1), cb.shape[2]), cb.dtype)
    hist_scr[...] = jnp.concatenate([zeros, cb], axis=1)


def _proj_conv_kernel(x_ref, cb_ref, ng_ref, wqkv_ref, wz_ref, wab_ref, cw_ref, gp_ref,
                      qkv_ref, z_ref, gb_ref, c_ref, hist_scr, *, bb, tl):
    @pl.when(pl.program_id(1) == 0)
    def _():
        _init_history(hist_scr, cb_ref)

    xn = _rms(x_ref[...].reshape(bb * tl, D_MODEL), ng_ref[...]).astype(BF16)
    _run(_proj_conv_stages(xn, bb, tl, hist_scr, c_ref, wqkv_ref, wz_ref, wab_ref, cw_ref, gp_ref,
                           qkv_ref, z_ref, gb_ref))


def _proj_consts(w):
    return [w["a_norm"], w["w_qkv"], w["w_z"], w["w_ab"], w["conv_w"], w["gate_prm"]]


def _proj_conv(x, cbuf, w, bb, tl, act_dtype):
    b, l, _ = x.shape
    tok = lambda n: pl.BlockSpec((bb, tl, n), lambda i, j: (i, j, 0))
    hist = pl.BlockSpec((bb, CONV_W - 1, A_CONV_CH), lambda i, j: (i, 0, 0))
    consts = _proj_consts(w)
    return pl.pallas_call(
        functools.partial(_proj_conv_kernel, bb=bb, tl=tl),
        grid=(b // bb, l // tl),
        in_specs=[tok(D_MODEL), hist] + [_const_spec(c) for c in consts],
        out_specs=[tok(A_CONV_CH), tok(A_VD), tok(2 * LANES), hist],
        out_shape=[jax.ShapeDtypeStruct((b, l, A_CONV_CH), act_dtype),
                   jax.ShapeDtypeStruct((b, l, A_VD), act_dtype),
                   jax.ShapeDtypeStruct((b, l, 2 * LANES), F32),
                   jax.ShapeDtypeStruct((b, CONV_W - 1, A_CONV_CH), F32)],
        scratch_shapes=[pltpu.VMEM((bb, SUBLANES, A_CONV_CH), F32)],
        compiler_params=_params(2),
        name="proj_conv",
    )(x, cbuf, *consts)


def _delta_scratch(tl, c):
    nc = tl // c
    gp = min(LANES // c, A_HEADS)
    mm_dtype = BF16 if c % 16 == 0 else F32
    return [pltpu.VMEM((nc, A_HEADS, 2 * c, A_DK), mm_dtype),
            pltpu.VMEM((nc, A_HEADS, c, A_DV), F32),
            pltpu.VMEM((nc, A_HEADS, c, A_DK), mm_dtype),
            pltpu.VMEM((nc, A_HEADS // gp, c, gp * c), mm_dtype)]


def _delta_stages(qkv_ref, gb_ref, s_in_ref, o_ref, s_ref,
                  wq_scr, ub_scr, kd_scr, qk_scr, *, tl, c, carry, s1_group):
    gp = min(LANES // c, A_HEADS)
    npk = A_HEADS // gp
    pw = gp * c
    nc = tl // c
    st = (lambda n, h: (0, h)) if carry else (lambda n, h: (n, h))

    ri = lax.broadcasted_iota(jnp.int32, (c, pw), 0)
    li = lax.broadcasted_iota(jnp.int32, (c, pw), 1)
    cj = li % c
    blk = li // c
    incl = ri >= cj
    strict = ri > cj
    eye = (ri == cj).astype(F32)
    tri = (lax.broadcasted_iota(jnp.int32, (c, c), 0) >= lax.broadcasted_iota(jnp.int32, (c, c), 1))
    tri = tri.astype(F32).astype(BF16)
    wide_blk = lax.broadcasted_iota(jnp.int32, (c, gp * A_DK), 1) // A_DK

    def lanes_of(cols, width):
        sel = blk if width == pw else wide_blk
        out = jnp.broadcast_to(cols[-1], (c, width))
        for i in range(gp - 2, -1, -1):
            out = jnp.where(sel == i, jnp.broadcast_to(cols[i], (c, width)), out)
        return out

    def block_diag(y):
        return jnp.concatenate([jnp.where(blk == i, y, 0.0) for i in range(gp)], axis=0).astype(BF16)

    def block_rows(parts):
        w = parts[0].shape[1]
        wide = jnp.concatenate(parts, axis=1)
        sel = lax.broadcasted_iota(jnp.int32, (c, gp * w), 1) // w
        return jnp.concatenate([jnp.where(sel == i, wide, 0.0) for i in range(gp)], axis=0).astype(BF16)

    gc = {}
    for g0 in range(0, nc, s1_group):
        chunks = range(g0, min(g0 + s1_group, nc))
        items = [(n, p) for n in chunks for p in range(npk)]
        for n in chunks:
            ga, gb1, gb2 = _split3(gb_ref[0, n * c:(n + 1) * c, :LANES])
            gc[n] = _dot(tri, ga) + (_dot(tri, gb1) + _dot(tri, gb2))
        gct = {n: gc[n].T for n in chunks}
        decay, a_neg, t_mat = {}, {}, {}
        for n, p in items:
            gcol = lanes_of([gc[n][:, p * gp + i:p * gp + i + 1] for i in range(gp)], pw)
            grow = jnp.concatenate([gct[n][p * gp + i:p * gp + i + 1, :] for i in range(gp)], axis=1)
            decay[n, p] = jnp.where(incl, jnp.exp(jnp.where(incl, gcol - grow, 0.0)), 0.0)
        yield
        for n, p in items:
            rows = slice(n * c, (n + 1) * c)
            kcols = slice(A_KD + p * gp * A_DK, A_KD + (p + 1) * gp * A_DK)
            k_pk = qkv_ref[0, rows, kcols]
            q_pk = qkv_ref[0, rows, p * gp * A_DK:(p + 1) * gp * A_DK]
            beta_w = lanes_of([gb_ref[0, rows, LANES + p * gp + i:LANES + p * gp + i + 1] for i in range(gp)],
                              gp * A_DK)
            k32 = k_pk.astype(F32)
            k_rows = jnp.concatenate([jnp.where(wide_blk == i, k32, 0.0) for i in range(gp)],
                                     axis=0).astype(BF16)
            kk = _dot_nt((k32 * beta_w).astype(BF16), k_rows)
            qk = _dot_nt(q_pk.astype(BF16), k_rows)
            a_neg[n, p] = -jnp.where(strict, kk * decay[n, p], 0.0)
            qk_scr[n, p] = jnp.where(incl, qk * decay[n, p], 0.0).astype(qk_scr.dtype)
        yield
        pw_mat = dict(a_neg)
        for key in items:
            t_mat[key] = eye + a_neg[key]
        m = 2
        while m < c:
            for key in items:
                pw_mat[key] = _dot(pw_mat[key].astype(BF16), block_diag(pw_mat[key]))
            yield
            for key in items:
                t_mat[key] = t_mat[key] + _dot(t_mat[key].astype(BF16), block_diag(pw_mat[key]))
            yield
            m *= 2
        for n, p in items:
            rows = slice(n * c, (n + 1) * c)
            xs = []
            for i in range(gp):
                h = p * gp + i
                k = qkv_ref[0, rows, A_KD + h * A_DK:A_KD + (h + 1) * A_DK].astype(F32)
                q = qkv_ref[0, rows, h * A_DK:(h + 1) * A_DK].astype(F32)
                v = qkv_ref[0, rows, 2 * A_KD + h * A_DV:2 * A_KD + (h + 1) * A_DV].astype(F32)
                gcol = gc[n][:, h:h + 1]
                bcol = gb_ref[0, rows, LANES + h:LANES + h + 1]
                egc = jnp.exp(gcol)
                kbeta = k * bcol
                xs.append(jnp.concatenate([v * bcol, kbeta * egc], axis=1))
                wq_scr[n, h, c:, :] = (q * egc).astype(wq_scr.dtype)
                kd_scr[n, h] = (k * jnp.exp(gc[n][c - 1:c, h:h + 1] - gcol)).astype(kd_scr.dtype)
            sol = _dot(t_mat[n, p].astype(BF16), block_rows(xs))
            for i in range(gp):
                h = p * gp + i
                ub_scr[n, h] = sol[:, i * 2 * A_DV:i * 2 * A_DV + A_DV]
                wq_scr[n, h, :c, :] = sol[:, i * 2 * A_DV + A_DV:(i + 1) * 2 * A_DV].astype(wq_scr.dtype)
        yield

    heads = range(A_HEADS)
    for n in range(nc):
        s = [(s_ref if carry else s_in_ref)[st(n, h)] for h in heads]
        s16 = [s[h].astype(BF16) for h in heads]
        wq = [_dot(wq_scr[n, h].astype(BF16), s16[h]) for h in heads]
        yield
        u = [ub_scr[n, h] - wq[h][:c] for h in heads]
        for p in range(npk):
            hs = [p * gp + i for i in range(gp)]
            o_pk = _dot(qk_scr[n, p].astype(BF16), block_rows([u[h] for h in hs]))
            o_ref[0, n * c:(n + 1) * c, p * gp * A_DV:(p + 1) * gp * A_DV] = (
                jnp.concatenate([wq[h][c:] for h in hs], axis=1) + o_pk)
        ku = [_dot_tn(kd_scr[n, h].astype(BF16), u[h].astype(BF16)) for h in heads]
        for h in heads:
            s_ref[st(n, h)] = s[h] * jnp.exp(gc[n][c - 1:c, h:h + 1]) + ku[h]
        yield


def _delta_kernel(qkv_ref, gb_ref, s0_ref, o_ref, s_ref, *scratch, tl, c, carry):
    if carry:
        @pl.when(pl.program_id(1) == 0)
        def _():
            s_ref[...] = s0_ref[...]
    _run(_delta_stages(qkv_ref, gb_ref, s0_ref, o_ref, s_ref, *scratch,
                       tl=tl, c=c, carry=carry, s1_group=tl // c))


def _delta(qkv, gb, s0, tl, c, carry):
    b, l, _ = qkv.shape
    assert carry or l == tl
    tok = lambda n: pl.BlockSpec((1, tl, n), lambda i, j: (i, j, 0))
    state = pl.BlockSpec((1 if carry else tl // c, A_HEADS, A_DK, A_DV), lambda i, j: (i, 0, 0, 0))
    return pl.pallas_call(
        functools.partial(_delta_kernel, tl=tl, c=c, carry=carry),
        grid=(b, l // tl),
        in_specs=[tok(A_CONV_CH), tok(2 * LANES), state],
        out_specs=[tok(A_VD), state],
        out_shape=[jax.ShapeDtypeStruct((b, l, A_VD), F32), jax.ShapeDtypeStruct(s0.shape, F32)],
        scratch_shapes=_delta_scratch(tl, c),
        compiler_params=_params(2),
        name="delta",
    )(qkv, gb, s0)


def _mixer_a_kernel(x_ref, cb_ref, s0_ref, ng_ref, wqkv_ref, wz_ref, wab_ref, cw_ref, gp_ref,
                    o_ref, z_ref, s_ref, c_ref, hist_scr, qkv_a, gb_a, qkv_b, gb_b, *scratch,
                    tl, c, nt, n_tiles):
    k = pl.program_id(0)
    tile_p = jnp.minimum(k, n_tiles - 1)
    tile_d = jnp.maximum(k - 1, 0)

    @pl.when(k == 0)
    def _():
        qkv_b[...] = jnp.zeros_like(qkv_b)
        gb_b[...] = jnp.zeros_like(gb_b)

    @pl.when(tile_p % nt == 0)
    def _():
        _init_history(hist_scr, cb_ref)

    @pl.when(tile_d % nt == 0)
    def _():
        s_ref[...] = s0_ref[...]

    def body(write, read):
        xn = _rms(x_ref[0], ng_ref[...]).astype(BF16)
        proj = _proj_conv_stages(xn, 1, tl, hist_scr, c_ref, wqkv_ref, wz_ref, wab_ref, cw_ref, gp_ref,
                                 write[0], z_ref, write[1])
        delta = _delta_stages(*read, s0_ref, o_ref, s_ref, *scratch, tl=tl, c=c, carry=True,
                              s1_group=MIXER_S1_GROUP)
        _weave(delta, proj, WEAVE_EVERY)

    @pl.when(k % 2 == 0)
    def _():
        body((qkv_a, gb_a), (qkv_b, gb_b))

    @pl.when(k % 2 == 1)
    def _():
        body((qkv_b, gb_b), (qkv_a, gb_a))


def _mixer_a(x, cbuf, s0, w, tl, c):
    b, l, _ = x.shape
    nt = l // tl
    n_tiles = b * nt
    p_idx = lambda k: jnp.minimum(k, n_tiles - 1)
    d_idx = lambda k: jnp.maximum(k - 1, 0)
    consts = _proj_consts(w)
    hand_off = [pltpu.VMEM((1, tl, A_CONV_CH), BF16), pltpu.VMEM((1, tl, 2 * LANES), F32)]
    return pl.pallas_call(
        functools.partial(_mixer_a_kernel, tl=tl, c=c, nt=nt, n_tiles=n_tiles),
        grid=(n_tiles + 1,),
        in_specs=[pl.BlockSpec((1, tl, D_MODEL), lambda k: (p_idx(k) // nt, p_idx(k) % nt, 0)),
                  pl.BlockSpec((1, CONV_W - 1, A_CONV_CH), lambda k: (p_idx(k) // nt, 0, 0)),
                  pl.BlockSpec((1, A_HEADS, A_DK, A_DV), lambda k: (d_idx(k) // nt, 0, 0, 0))]
                 + [_const_spec(cst) for cst in consts],
        out_specs=[pl.BlockSpec((1, tl, A_VD), lambda k: (d_idx(k) // nt, d_idx(k) % nt, 0)),
                   pl.BlockSpec((1, tl, A_VD), lambda k: (p_idx(k) // nt, p_idx(k) % nt, 0)),
                   pl.BlockSpec((1, A_HEADS, A_DK, A_DV), lambda k: (d_idx(k) // nt, 0, 0, 0)),
                   pl.BlockSpec((1, CONV_W - 1, A_CONV_CH), lambda k: (p_idx(k) // nt, 0, 0))],
        out_shape=[jax.ShapeDtypeStruct((b, l, A_VD), F32),
                   jax.ShapeDtypeStruct((b, l, A_VD), BF16),
                   jax.ShapeDtypeStruct((b, A_HEADS, A_DK, A_DV), F32),
                   jax.ShapeDtypeStruct((b, CONV_W - 1, A_CONV_CH), F32)],
        scratch_shapes=[pltpu.VMEM((1, SUBLANES, A_CONV_CH), F32)] + hand_off + hand_off
                       + _delta_scratch(tl, c),
        compiler_params=_params(1),
        name="mixer_a",
    )(x, cbuf, s0, *consts)


def _post_stages(get_mix, res_ref, pe_ref, wmix_ref, fg_ref, wgu_ref, wd_ref,
                 pg_ref, wpg_ref, wpp_ref, n_chunks, finish):
    h = res_ref[...] + _dot(get_mix(), wmix_ref[...])
    xn = _rms(h, fg_ref[...]).astype(BF16)
    yield
    fc = FFN_HIDDEN // n_chunks
    acc = h
    for i in range(n_chunks):
        gt = _dot(xn, wgu_ref[:, i * fc:(i + 1) * fc])
        up = _dot(xn, wgu_ref[:, FFN_HIDDEN + i * fc:FFN_HIDDEN + (i + 1) * fc])
        yield
        acc = acc + _dot((_silu(gt) * up).astype(BF16), wd_ref[i * fc:(i + 1) * fc, :])
        yield
    h2 = acc
    gate = _sigmoid(_dot(_rms(h2, pg_ref[...]).astype(BF16), wpg_ref[...]))
    yield
    finish(h2 + _dot(pe_ref[...].astype(BF16), wpp_ref[...]) * gate)


def _post_mid_kernel(o_nxt_ref, z_nxt_ref, o_first_ref, z_first_ref, res_ref, pe_ref, og_ref,
                     wmix_ref, fg_ref, wgu_ref, wd_ref, pg_ref, wpg_ref, wpp_ref,
                     kvn_ref, kvw_ref, bn_ref, wq_ref, h_ref, q_ref, kv_ref, mix_scr, *, n_chunks):
    def gate_into_scratch(o_ref, z_ref):
        for hd in range(A_HEADS):
            vs = slice(hd * A_DV, (hd + 1) * A_DV)
            oh = o_ref[:, vs]
            oh = oh * lax.rsqrt(jnp.mean(oh * oh, axis=-1, keepdims=True) + EPS) * og_ref[...]
            mix_scr[:, vs] = (oh * _silu(z_ref[:, vs].astype(F32))).astype(BF16)

    @pl.when(pl.program_id(0) == 0)
    def _():
        gate_into_scratch(o_first_ref, z_first_ref)

    def finish(h3):
        h_ref[...] = h3
        kv_ref[...] = _dot(_rms(h3, kvn_ref[...]).astype(BF16), kvw_ref[...])
        q = _dot(_rms(h3, bn_ref[...]).astype(BF16), wq_ref[...]) * (B_HEAD_DIM ** -0.5)
        q_ref[...] = q.astype(q_ref.dtype)

    stages = _post_stages(lambda: mix_scr[...], res_ref, pe_ref, wmix_ref, fg_ref, wgu_ref, wd_ref,
                          pg_ref, wpg_ref, wpp_ref, n_chunks, finish)
    next(stages)
    gate_into_scratch(o_nxt_ref, z_nxt_ref)
    _run(stages)


def _final_norm(fn_ref, y_ref):
    def finish(h3):
        y_ref[...] = _rms(h3, fn_ref[...])
    return finish


def _post_last_kernel(mix_ref, res_ref, pe_ref, wmix_ref, fg_ref, wgu_ref, wd_ref,
                      pg_ref, wpg_ref, wpp_ref, fn_ref, y_ref, *, n_chunks):
    _run(_post_stages(lambda: mix_ref[...], res_ref, pe_ref, wmix_ref, fg_ref, wgu_ref, wd_ref,
                      pg_ref, wpg_ref, wpp_ref, n_chunks, _final_norm(fn_ref, y_ref)))


def _post(mix, res, pe, layer, common, tail, tm, last, q_dtype=BF16):
    m = res.shape[0]
    row = lambda n: pl.BlockSpec((tm, n), lambda i: (i, 0))
    weights = list(common) + list(tail)
    pe_spec = pl.BlockSpec((None, tm, PLE_DIM), lambda i: (layer, i, 0))
    if last:
        rows_in, gain, scratch = [mix], [], []
        mix_specs = [row(mix.shape[1])]
        kern = _post_last_kernel
        out_specs = [row(D_MODEL)]
        out_shape = [jax.ShapeDtypeStruct((m, D_MODEL), F32)]
    else:
        o_raw, z, out_g = mix
        rows_in, gain = [o_raw, z, o_raw, z], [out_g]
        nxt = lambda n: pl.BlockSpec((tm, n), lambda i: (jnp.minimum(i + 1, m // tm - 1), 0))
        first = lambda n: pl.BlockSpec((tm, n), lambda i: (0, 0), pipeline_mode=pl.Buffered(1))
        mix_specs = [nxt(A_VD), nxt(A_VD), first(A_VD), first(A_VD)]
        scratch = [pltpu.VMEM((tm, A_VD), BF16)]
        kern = _post_mid_kernel
        out_specs = [row(D_MODEL), row(B_QD), row(2 * B_KVD)]
        out_shape = [jax.ShapeDtypeStruct((m, D_MODEL), F32),
                     jax.ShapeDtypeStruct((m, B_QD), q_dtype),
                     jax.ShapeDtypeStruct((m, 2 * B_KVD), F32)]
    in_specs = (mix_specs + [row(D_MODEL), pe_spec] + [_const_spec(a) for a in gain]
                + [_const_spec(a, lyr) for a, lyr in weights])
    return pl.pallas_call(
        functools.partial(kern, n_chunks=FFN_CHUNKS),
        grid=(m // tm,),
        in_specs=in_specs,
        out_specs=out_specs,
        out_shape=out_shape,
        scratch_shapes=scratch,
        compiler_params=_params(1),
        name="post_last" if last else "post_mid",
    )(*rows_in, res, pe, *gain, *[a for a, _ in weights])


def _attn_block_stages(sink_ref, q_at, kv_own, kv_prev, blk_idx, put):
    rows = B_GROUP * WINDOW
    k2 = jnp.concatenate([kv_own[:, :B_KVD], kv_prev[:, :B_KVD]], axis=0).astype(BF16)
    v2 = jnp.concatenate([kv_own[:, B_KVD:], kv_prev[:, B_KVD:]], axis=0).astype(BF16)
    qi = lax.broadcasted_iota(jnp.int32, (rows, WINDOW), 0) % WINDOW
    ci = lax.broadcasted_iota(jnp.int32, (rows, WINDOW), 1)
    own = ci <= qi
    prev_ok = (ci > qi) & ((blk_idx - 1) * WINDOW + ci >= 0)
    grp = lax.broadcasted_iota(jnp.int32, (rows, 1), 0) // WINDOW
    kv_heads = range(B_KV_HEADS)
    sc = []
    for j in kv_heads:
        qs = jnp.concatenate([q_at(j * B_GROUP + g) for g in range(B_GROUP)], axis=0)
        sc.append(_dot_nt(qs, k2[:, j * B_HEAD_DIM:(j + 1) * B_HEAD_DIM]))
    yield
    pv = []
    for j in kv_heads:
        sink = jnp.full((rows, 1), sink_ref[j * B_GROUP + B_GROUP - 1], F32)
        for g in range(B_GROUP - 2, -1, -1):
            sink = jnp.where(grp == g, sink_ref[j * B_GROUP + g], sink)
        s = jnp.where(own, sc[j][:, :WINDOW], jnp.where(prev_ok, sc[j][:, WINDOW:], NEG_INF))
        mx = jnp.maximum(jnp.max(s, axis=-1, keepdims=True), sink)
        p = jnp.exp(s - mx)
        den = jnp.sum(p, axis=-1, keepdims=True) + jnp.exp(sink - mx)
        p2 = jnp.concatenate([jnp.where(own, p, 0.0), jnp.where(own, 0.0, p)], axis=1).astype(BF16)
        pv.append((p2, den))
    outs = []
    for j in kv_heads:
        o = _dot(pv[j][0], v2[:, j * B_HEAD_DIM:(j + 1) * B_HEAD_DIM]) / pv[j][1]
        outs += [o[g * WINDOW:(g + 1) * WINDOW, :] for g in range(B_GROUP)]
    put(jnp.concatenate(outs, axis=1))
    yield


def _attn_tile_stages(sink_ref, q_ref, kv_ref, kvp_ref, first_blk, att_ref, tm):
    for j in range(tm // WINDOW):
        r = slice(j * WINDOW, (j + 1) * WINDOW)
        kv_prev = kvp_ref[...] if j == 0 else kv_ref[(j - 1) * WINDOW:j * WINDOW, :]

        def put(o, r=r):
            att_ref[r, :] = o.astype(att_ref.dtype)

        yield from _attn_block_stages(
            sink_ref, lambda h, r=r: q_ref[r, h * B_HEAD_DIM:(h + 1) * B_HEAD_DIM],
            kv_ref[r, :], kv_prev, first_blk + j, put)


def _layer_b_kernel(sink_ref, q_ref, kv_ref, kvp_ref, res_ref, pe_ref, wmix_ref, fg_ref, wgu_ref, wd_ref,
                    pg_ref, wpg_ref, wpp_ref, fn_ref, y_ref, att_a, att_b, *, tm, nt, n_tiles):
    k = pl.program_id(0)
    first_blk = (jnp.minimum(k, n_tiles - 1) % nt) * (tm // WINDOW)

    @pl.when(k == 0)
    def _():
        att_b[...] = jnp.zeros_like(att_b)

    def body(write, read):
        attn = _attn_tile_stages(sink_ref, q_ref, kv_ref, kvp_ref, first_blk, write, tm)
        post = _post_stages(lambda: read[...], res_ref, pe_ref, wmix_ref, fg_ref, wgu_ref, wd_ref,
                            pg_ref, wpg_ref, wpp_ref, FFN_CHUNKS, _final_norm(fn_ref, y_ref))
        _weave(attn, post, 1)

    @pl.when(k % 2 == 0)
    def _():
        body(att_a, att_b)

    @pl.when(k % 2 == 1)
    def _():
        body(att_b, att_a)


def _layer_b(q, kv, sinks, res, pe, layer, common, tail, tm, l):
    m = res.shape[0]
    nt = l // tm
    n_tiles = m // tm
    bpt = tm // WINDOW
    p_idx = lambda k: jnp.minimum(k, n_tiles - 1)
    d_idx = lambda k: jnp.maximum(k - 1, 0)
    weights = list(common) + list(tail)
    return pl.pallas_call(
        functools.partial(_layer_b_kernel, tm=tm, nt=nt, n_tiles=n_tiles),
        grid=(n_tiles + 1,),
        in_specs=[pl.BlockSpec(memory_space=pltpu.SMEM),
                  pl.BlockSpec((tm, B_QD), lambda k: (p_idx(k), 0)),
                  pl.BlockSpec((tm, 2 * B_KVD), lambda k: (p_idx(k), 0)),
                  pl.BlockSpec((WINDOW, 2 * B_KVD), lambda k: (jnp.maximum(p_idx(k) * bpt - 1, 0), 0)),
                  pl.BlockSpec((tm, D_MODEL), lambda k: (d_idx(k), 0)),
                  pl.BlockSpec((None, tm, PLE_DIM), lambda k: (layer, d_idx(k), 0))]
                 + [_const_spec(a, lyr) for a, lyr in weights],
        out_specs=pl.BlockSpec((tm, D_MODEL), lambda k: (d_idx(k), 0)),
        out_shape=jax.ShapeDtypeStruct((m, D_MODEL), F32),
        scratch_shapes=[pltpu.VMEM((tm, B_QD), BF16), pltpu.VMEM((tm, B_QD), BF16)],
        compiler_params=_params(1),
        name="layer_b",
    )(sinks, q, kv, kv, res, pe, *[a for a, _ in weights])


def _attn_sample_kernel(sink_ref, q_ref, kv_ref, ck_ref, cv_ref, o_ref, *, bb, t, nbuf):
    rows = B_GROUP * t
    qi = lax.broadcasted_iota(jnp.int32, (rows, nbuf + t), 0) % t
    ki = lax.broadcasted_iota(jnp.int32, (rows, nbuf + t), 1)
    rel = qi + nbuf - ki
    mask = (rel >= 0) & (rel < WINDOW)
    grp = lax.broadcasted_iota(jnp.int32, (rows, 1), 0) // t
    items = [(b, j) for b in range(bb) for j in range(B_KV_HEADS)]
    sink_col = {}
    for j in range(B_KV_HEADS):
        col = jnp.full((rows, 1), sink_ref[j * B_GROUP + B_GROUP - 1], F32)
        for g in range(B_GROUP - 2, -1, -1):
            col = jnp.where(grp == g, sink_ref[j * B_GROUP + g], col)
        sink_col[j] = col
    keys, vals = {}, {}
    for b in range(bb):
        kv = kv_ref[b]
        keys[b] = jnp.concatenate([ck_ref[b], kv[:, :B_KVD]], axis=0).astype(BF16)
        vals[b] = jnp.concatenate([cv_ref[b], kv[:, B_KVD:]], axis=0).astype(BF16)
    s = {}
    for b, j in items:
        qs = jnp.concatenate(
            [q_ref[b, :, (j * B_GROUP + g) * B_HEAD_DIM:(j * B_GROUP + g + 1) * B_HEAD_DIM]
             for g in range(B_GROUP)], axis=0).astype(BF16)
        s[b, j] = jnp.where(mask, _dot_nt(qs, keys[b][:, j * B_HEAD_DIM:(j + 1) * B_HEAD_DIM]), NEG_INF)
    p, den = {}, {}
    for b, j in items:
        mx = jnp.maximum(jnp.max(s[b, j], axis=-1, keepdims=True), sink_col[j])
        p[b, j] = jnp.exp(s[b, j] - mx)
        den[b, j] = jnp.sum(p[b, j], axis=-1, keepdims=True) + jnp.exp(sink_col[j] - mx)
    o = {}
    for b, j in items:
        o[b, j] = _dot(p[b, j].astype(BF16), vals[b][:, j * B_HEAD_DIM:(j + 1) * B_HEAD_DIM]) / den[b, j]
    for b in range(bb):
        o_ref[b] = jnp.concatenate(
            [o[b, j][g * t:(g + 1) * t, :] for j in range(B_KV_HEADS) for g in range(B_GROUP)],
            axis=1).astype(o_ref.dtype)


def _attn_sample(q, kv, ck, cv, sinks, bb):
    b, t, _ = q.shape
    nbuf = ck.shape[1]
    per_b = lambda r, n: pl.BlockSpec((bb, r, n), lambda i: (i, 0, 0))
    return pl.pallas_call(
        functools.partial(_attn_sample_kernel, bb=bb, t=t, nbuf=nbuf),
        grid=(b // bb,),
        in_specs=[pl.BlockSpec(memory_space=pltpu.SMEM),
                  per_b(t, B_QD), per_b(t, 2 * B_KVD), per_b(nbuf, B_KVD), per_b(nbuf, B_KVD)],
        out_specs=per_b(t, B_QD),
        out_shape=jax.ShapeDtypeStruct((b, t, B_QD), BF16),
        compiler_params=_params(1),
        name="attn_sample",
    )(sinks, q, kv, ck, cv)


def _pad_lanes(v):
    return jnp.pad(v, ((0, 0), (0, LANES - v.shape[1])))


def _prepare(p):
    w_in = p["a_w_in"][0]
    n_gate = A_CONV_CH + A_VD
    row = lambda v: v.reshape(1, -1).astype(F32)
    gu, down = p["ffn_w_gu"].astype(BF16), p["ffn_w_down"].astype(BF16)
    ple_gate, ple_proj = p["ple_w_gate"].astype(BF16), p["ple_w_proj"].astype(BF16)

    def post_common(i, w_mix):
        return [(w_mix.astype(BF16), None), (row(p["ffn_norm"][i]), None), (gu, i), (down, i),
                (row(p["ple_norm"][i]), None), (ple_gate, i), (ple_proj, i)]

    return {
        "a_norm": row(p["a_norm"][0]),
        "w_qkv": w_in[:, :A_CONV_CH].astype(BF16),
        "w_z": w_in[:, A_CONV_CH:n_gate].astype(BF16),
        "w_ab": jnp.concatenate([_pad_lanes(w_in[:, n_gate:n_gate + A_HEADS]),
                                 _pad_lanes(w_in[:, n_gate + A_HEADS:])], axis=1).astype(BF16),
        "conv_w": p["a_conv_w"][0],
        "gate_prm": jnp.concatenate([_pad_lanes(row(p["a_a_log"][0])),
                                     _pad_lanes(row(p["a_dt_bias"][0]))], axis=0),
        "out_g": row(p["a_out_norm"][0]),
        "post0": post_common(0, p["a_w_out"][0]),
        "tail0": [(row(p["kv_norm"]), None), (p["kv_w"].astype(BF16), None),
                  (row(p["b_norm"][0]), None), (p["b_w_q"][0].astype(BF16), None)],
        "post1": post_common(1, p["b_w_o"][0]),
        "tail1": [(row(p["final_norm"]), None)],
        "sinks": p["b_sinks"][0].astype(F32),
    }


def _tiles(b, l, prompt):
    m = b * l
    c = min(CHUNK, l)
    if prompt:
        return dict(tm=min(512, m), mx_tl=min(512, l), c=c)
    return dict(tm=min(256, m), pc_bb=min(32, b), dl_tl=min(8 * l, m), c=c, at_bb=min(8, b))


def _trunk(x, pe, s_init, c_init, k_buf, v_buf, w):
    b, l, _ = x.shape
    m = b * l
    prompt = k_buf is None
    ts = _tiles(b, l, prompt)
    act_dtype = BF16 if prompt else F32
    if prompt:
        o, z, s_new, c_new = _mixer_a(x, c_init, s_init, w, ts["mx_tl"], ts["c"])
    else:
        qkv, z, gb, c_new = _proj_conv(x, c_init, w, ts["pc_bb"], l, act_dtype)
        grp = lambda a: a.reshape(m // ts["dl_tl"], ts["dl_tl"], a.shape[-1])
        o, s_new = _delta(grp(qkv), grp(gb), s_init, ts["dl_tl"], ts["c"], False)
    x2d = x.reshape(m, D_MODEL)
    pe2d = pe.reshape(pe.shape[0], m, PLE_DIM)
    h, q, kv = _post((o.reshape(m, -1), z.reshape(m, -1), w["out_g"]), x2d, pe2d, 0,
                     w["post0"], w["tail0"], ts["tm"], False, act_dtype)
    kv3 = kv.reshape(b, l, -1)
    heads = lambda a: a.reshape(a.shape[0], a.shape[1], B_KV_HEADS, B_HEAD_DIM)
    if prompt:
        y = _layer_b(q, kv, w["sinks"], h, pe2d, 1, w["post1"], w["tail1"], ts["tm"], l)
        last = kv3[:, l - WINDOW:]
        k_win, v_win = heads(last[:, :, :B_KVD]), heads(last[:, :, B_KVD:])
    else:
        nbuf = k_buf.shape[1]
        att = _attn_sample(q.reshape(b, l, -1), kv3, k_buf.reshape(b, nbuf, B_KVD),
                           v_buf.reshape(b, nbuf, B_KVD), w["sinks"], ts["at_bb"])
        k_win = jnp.concatenate([k_buf, heads(kv3[:, :, :B_KVD])], axis=1)[:, -nbuf:]
        v_win = jnp.concatenate([v_buf, heads(kv3[:, :, B_KVD:])], axis=1)[:, -nbuf:]
        (y,) = _post(att.reshape(m, -1), h, pe2d, 1, w["post1"], w["tail1"], ts["tm"], True)
    return y.reshape(b, l, D_MODEL), s_new[None], c_new[None], k_win, v_win


def kernel(x_prompt, x_sample, state_delta, state_conv, cache_k_win, cache_v_win, p_prompt, p_sample, a_norm, a_w_in, a_conv_w, a_a_log, a_dt_bias, a_out_norm, a_w_out, kv_norm, kv_w, b_norm, b_w_q, b_sinks, b_w_o, ffn_norm, ffn_w_gu, ffn_w_down, ple_norm, ple_w_proj, ple_w_gate, final_norm):
    w = _prepare({
        "a_norm": a_norm, "a_w_in": a_w_in, "a_conv_w": a_conv_w, "a_a_log": a_a_log,
        "a_dt_bias": a_dt_bias, "a_out_norm": a_out_norm, "a_w_out": a_w_out,
        "kv_norm": kv_norm, "kv_w": kv_w, "b_norm": b_norm, "b_w_q": b_w_q, "b_sinks": b_sinks,
        "b_w_o": b_w_o, "ffn_norm": ffn_norm, "ffn_w_gu": ffn_w_gu, "ffn_w_down": ffn_w_down,
        "ple_norm": ple_norm, "ple_w_proj": ple_w_proj, "ple_w_gate": ple_w_gate,
        "final_norm": final_norm,
    })
    bp = x_prompt.shape[0]
    s0 = jnp.zeros((bp, A_HEADS, A_DK, A_DV), F32)
    c0 = jnp.zeros((bp, CONV_W - 1, A_CONV_CH), x_prompt.dtype)
    y_p, sd_p, sc_p, kw_p, vw_p = _trunk(x_prompt, p_prompt, s0, c0, None, None, w)
    y_s, sd_s, sc_s, kw_s, vw_s = _trunk(x_sample, p_sample, state_delta[0], state_conv[0],
                                         cache_k_win, cache_v_win, w)
    return (y_p, y_s, sd_p, sd_s, sc_p, sc_s, kw_p, kw_s, vw_p, vw_s)
```

```python
import functools

import jax
import jax.numpy as jnp
from jax import lax
from jax.experimental import pallas as pl
from jax.experimental.pallas import tpu as pltpu

F32 = jnp.float32
BF16 = jnp.bfloat16

D_MODEL = 1024
A_HEADS = 8
A_DK = 128
A_DV = 128
A_KD = A_HEADS * A_DK
A_VD = A_HEADS * A_DV
A_CONV_CH = 2 * A_KD + A_VD
CONV_W = 4
CHUNK = 64
B_Q_HEADS = 16
B_KV_HEADS = 4
B_HEAD_DIM = 64
B_GROUP = B_Q_HEADS // B_KV_HEADS
B_QD = B_Q_HEADS * B_HEAD_DIM
B_KVD = B_KV_HEADS * B_HEAD_DIM
WINDOW = 128
FFN_HIDDEN = 2816
PLE_DIM = 256
EPS = 1e-6
L2_EPS = 1e-6
NEG_INF = -1e30

LANES = 128
SUBLANES = 8
VMEM_LIMIT_BYTES = 56 * 1024 * 1024
PROJ_SLAB = 512
WEAVE_EVERY = 2
MIXER_S1_GROUP = 4
FFN_CHUNKS = 2


def _rms(x, g):
    return x * lax.rsqrt(jnp.mean(x * x, axis=-1, keepdims=True) + EPS) * g


def _sigmoid(x):
    return 1.0 / (1.0 + jnp.exp(-x))


def _silu(x):
    return x * _sigmoid(x)


def _softplus(x):
    return jnp.maximum(x, 0.0) + jnp.log1p(jnp.exp(-jnp.abs(x)))


def _dot(a, b):
    return jnp.dot(a, b, preferred_element_type=F32)


def _dot_nt(a, b):
    return lax.dot_general(a, b, (((1,), (1,)), ((), ())), preferred_element_type=F32)


def _dot_tn(a, b):
    return lax.dot_general(a, b, (((0,), (0,)), ((), ())), preferred_element_type=F32)


def _split3(x):
    p0 = x.astype(BF16)
    r = x - p0.astype(F32)
    p1 = r.astype(BF16)
    p2 = (r - p1.astype(F32)).astype(BF16)
    return p0, p1, p2


def _const_spec(arr, layer=None):
    if layer is None:
        n = arr.ndim
        return pl.BlockSpec(arr.shape, lambda *_: (0,) * n, pipeline_mode=pl.Buffered(1))
    n = arr.ndim - 1
    return pl.BlockSpec((None,) + arr.shape[1:], lambda *_: (layer,) + (0,) * n,
                        pipeline_mode=pl.Buffered(1))


def _params(n_axes):
    return pltpu.CompilerParams(dimension_semantics=("arbitrary",) * n_axes,
                                vmem_limit_bytes=VMEM_LIMIT_BYTES)


def _run(gen):
    for _ in gen:
        pass


def _weave(main, side, every):
    next(side, None)
    for i, _ in enumerate(main):
        if (i + 1) % every == 0:
            next(side, None)
    _run(side)


def _proj_conv_stages(xn, bb, tl, hist_scr, c_ref, wqkv_ref, wz_ref, wab_ref, cw_ref, gp_ref,
                      qkv_out, z_out, gb_out):
    hist = CONV_W - 1
    sub = SUBLANES
    g = tl // sub
    row_in_group = lax.broadcasted_iota(jnp.int32, (1, 1, sub, 1), 2)
    for s in range(A_CONV_CH // PROJ_SLAB):
        c0 = s * PROJ_SLAB
        cols = slice(c0, c0 + PROJ_SLAB)
        p = _dot(xn, wqkv_ref[:, cols]).reshape(bb, tl, PROJ_SLAB)
        groups = jnp.concatenate([hist_scr[:, :, cols][:, None], p.reshape(bb, g, sub, PROJ_SLAB)], axis=1)
        conv = p * cw_ref[hist:hist + 1, cols]
        for j in range(1, CONV_W):
            rot = pltpu.roll(groups.reshape(bb * (g + 1), sub, PROJ_SLAB), j, axis=1)
            rot = rot.reshape(bb, g + 1, sub, PROJ_SLAB)
            shifted = jnp.where(row_in_group < j, rot[:, :-1], rot[:, 1:])
            conv = conv + shifted.reshape(bb, tl, PROJ_SLAB) * cw_ref[hist - j:hist - j + 1, cols]
        hist_scr[:, :, cols] = p[:, tl - sub:, :]
        tail = p[:, tl - hist:, :]
        c_ref[:, :, cols] = tail
        act = _silu(conv)
        if c0 < 2 * A_KD:
            scale = A_DK ** -0.5 if c0 < A_KD else 1.0
            for h in range(PROJ_SLAB // A_DK):
                a = act[:, :, h * A_DK:(h + 1) * A_DK]
                a = a * lax.rsqrt(jnp.sum(a * a, axis=-1, keepdims=True) + L2_EPS) * scale
                qkv_out[:, :, c0 + h * A_DK:c0 + (h + 1) * A_DK] = a.astype(qkv_out.dtype)
        else:
            qkv_out[:, :, cols] = act.astype(qkv_out.dtype)
        yield
    for s in range(A_VD // PROJ_SLAB):
        cols = slice(s * PROJ_SLAB, (s + 1) * PROJ_SLAB)
        z = _dot(xn, wz_ref[:, cols])
        z_out[:, :, cols] = z.reshape(bb, tl, PROJ_SLAB).astype(z_out.dtype)
        yield
    ab = _dot(xn, wab_ref[...])
    g = -jnp.exp(gp_ref[0:1, :]) * _softplus(ab[:, :LANES] + gp_ref[1:2, :])
    beta = _sigmoid(ab[:, LANES:])
    gb_out[...] = jnp.concatenate([g, beta], axis=1).reshape(bb, tl, 2 * LANES)
    yield


def _init_history(hist_scr, cb_ref):
    cb = cb_ref[...]
    zeros = jnp.zeros((cb.shape[0], SUBLANES - (CONV_W - 1), cb.shape[2]), cb.dtype)
    hist_scr[...] = jnp.concatenate([zeros, cb], axis=1)


def _proj_conv_kernel(x_ref, cb_ref, ng_ref, wqkv_ref, wz_ref, wab_ref, cw_ref, gp_ref,
                      qkv_ref, z_ref, gb_ref, c_ref, hist_scr, *, bb, tl):
    @pl.when(pl.program_id(1) == 0)
    def _():
        _init_history(hist_scr, cb_ref)

    xn = _rms(x_ref[...].reshape(bb * tl, D_MODEL), ng_ref[...]).astype(BF16)
    _run(_proj_conv_stages(xn, bb, tl, hist_scr, c_ref, wqkv_ref, wz_ref, wab_ref, cw_ref, gp_ref,
                           qkv_ref, z_ref, gb_ref))


def _proj_consts(w):
    return [w["a_norm"], w["w_qkv"], w["w_z"], w["w_ab"], w["conv_w"], w["gate_prm"]]


def _proj_conv(x, cbuf, w, bb, tl, act_dtype):
    b, l, _ = x.shape
    tok = lambda n: pl.BlockSpec((bb, tl, n), lambda i, j: (i, j, 0))
    hist = pl.BlockSpec((bb, CONV_W - 1, A_CONV_CH), lambda i, j: (i, 0, 0))
    consts = _proj_consts(w)
    return pl.pallas_call(
        functools.partial(_proj_conv_kernel, bb=bb, tl=tl),
        grid=(b // bb, l // tl),
        in_specs=[tok(D_MODEL), hist] + [_const_spec(c) for c in consts],
        out_specs=[tok(A_CONV_CH), tok(A_VD), tok(2 * LANES), hist],
        out_shape=[jax.ShapeDtypeStruct((b, l, A_CONV_CH), act_dtype),
                   jax.ShapeDtypeStruct((b, l, A_VD), act_dtype),
                   jax.ShapeDtypeStruct((b, l, 2 * LANES), F32),
                   jax.ShapeDtypeStruct((b, CONV_W - 1, A_CONV_CH), F32)],
        scratch_shapes=[pltpu.VMEM((bb, SUBLANES, A_CONV_CH), F32)],
        compiler_params=_params(2),
        name="proj_conv",
    )(x, cbuf, *consts)


def _delta_scratch(tl, c):
    nc = tl // c
    gp = min(LANES // c, A_HEADS)
    mm_dtype = BF16 if c % 16 == 0 else F32
    return [pltpu.VMEM((nc, A_HEADS, 2 * c, A_DK), mm_dtype),
            pltpu.VMEM((nc, A_HEADS, c, A_DV), F32),
            pltpu.VMEM((nc, A_HEADS, c, A_DK), mm_dtype),
            pltpu.VMEM((nc, A_HEADS // gp, c, gp * c), mm_dtype)]


def _delta_stages(qkv_ref, gb_ref, s_in_ref, o_ref, s_ref,
                  wq_scr, ub_scr, kd_scr, qk_scr, *, tl, c, carry, s1_group):
    gp = min(LANES // c, A_HEADS)
    npk = A_HEADS // gp
    pw = gp * c
    nc = tl // c
    st = (lambda n, h: (0, h)) if carry else (lambda n, h: (n, h))

    ri = lax.broadcasted_iota(jnp.int32, (c, pw), 0)
    li = lax.broadcasted_iota(jnp.int32, (c, pw), 1)
    cj = li % c
    blk = li // c
    incl = ri >= cj
    strict = ri > cj
    eye = (ri == cj).astype(F32)
    tri = (lax.broadcasted_iota(jnp.int32, (c, c), 0) >= lax.broadcasted_iota(jnp.int32, (c, c), 1))
    tri = tri.astype(F32).astype(BF16)
    wide_blk = lax.broadcasted_iota(jnp.int32, (c, gp * A_DK), 1) // A_DK

    def lanes_of(cols, width):
        sel = blk if width == pw else wide_blk
        out = jnp.broadcast_to(cols[-1], (c, width))
        for i in range(gp - 2, -1, -1):
            out = jnp.where(sel == i, jnp.broadcast_to(cols[i], (c, width)), out)
        return out

    def block_diag(y):
        return jnp.concatenate([jnp.where(blk == i, y, 0.0) for i in range(gp)], axis=0).astype(BF16)

    def block_rows(parts):
        w = parts[0].shape[1]
        wide = jnp.concatenate(parts, axis=1)
        sel = lax.broadcasted_iota(jnp.int32, (c, gp * w), 1) // w
        return jnp.concatenate([jnp.where(sel == i, wide, 0.0) for i in range(gp)], axis=0).astype(BF16)

    gc = {}
    for g0 in range(0, nc, s1_group):
        chunks = range(g0, min(g0 + s1_group, nc))
        items = [(n, p) for n in chunks for p in range(npk)]
        for n in chunks:
            ga, gb1, gb2 = _split3(gb_ref[0, n * c:(n + 1) * c, :LANES])
            gc[n] = _dot(tri, ga) + (_dot(tri, gb1) + _dot(tri, gb2))
        gct = {n: gc[n].T for n in chunks}
        decay, a_neg, t_mat = {}, {}, {}
        for n, p in items:
            gcol = lanes_of([gc[n][:, p * gp + i:p * gp + i + 1] for i in range(gp)], pw)
            grow = jnp.concatenate([gct[n][p * gp + i:p * gp + i + 1, :] for i in range(gp)], axis=1)
            decay[n, p] = jnp.where(incl, jnp.exp(jnp.where(incl, gcol - grow, 0.0)), 0.0)
        yield
        for n, p in items:
            rows = slice(n * c, (n + 1) * c)
            kcols = slice(A_KD + p * gp * A_DK, A_KD + (p + 1) * gp * A_DK)
            k_pk = qkv_ref[0, rows, kcols]
            q_pk = qkv_ref[0, rows, p * gp * A_DK:(p + 1) * gp * A_DK]
            beta_w = lanes_of([gb_ref[0, rows, LANES + p * gp + i:LANES + p * gp + i + 1] for i in range(gp)],
                              gp * A_DK)
            k32 = k_pk.astype(F32)
            k_rows = jnp.concatenate([jnp.where(wide_blk == i, k32, 0.0) for i in range(gp)],
                                     axis=0).astype(BF16)
            kk = _dot_nt((k32 * beta_w).astype(BF16), k_rows)
            qk = _dot_nt(q_pk.astype(BF16), k_rows)
            a_neg[n, p] = -jnp.where(strict, kk * decay[n, p], 0.0)
            qk_scr[n, p] = jnp.where(incl, qk * decay[n, p], 0.0).astype(qk_scr.dtype)
        yield
        pw_mat = dict(a_neg)
        for key in items:
            t_mat[key] = eye + a_neg[key]
        m = 2
        while m < c:
            for key in items:
                pw_mat[key] = _dot(pw_mat[key].astype(BF16), block_diag(pw_mat[key]))
            yield
            for key in items:
                t_mat[key] = t_mat[key] + _dot(t_mat[key].astype(BF16), block_diag(pw_mat[key]))
            yield
            m *= 2
        for n, p in items:
            rows = slice(n * c, (n + 1) * c)
            xs = []
            for i in range(gp):
                h = p * gp + i
                k = qkv_ref[0, rows, A_KD + h * A_DK:A_KD + (h + 1) * A_DK].astype(F32)
                q = qkv_ref[0, rows, h * A_DK:(h + 1) * A_DK].astype(F32)
                v = qkv_ref[0, rows, 2 * A_KD + h * A_DV:2 * A_KD + (h + 1) * A_DV].astype(F32)
                gcol = gc[n][:, h:h + 1]
                bcol = gb_ref[0, rows, LANES + h:LANES + h + 1]
                egc = jnp.exp(gcol)
                kbeta = k * bcol
                xs.append(jnp.concatenate([v * bcol, kbeta * egc], axis=1))
                wq_scr[n, h, c:, :] = (q * egc).astype(wq_scr.dtype)
                kd_scr[n, h] = (k * jnp.exp(gc[n][c - 1:c, h:h + 1] - gcol)).astype(kd_scr.dtype)
            sol = _dot(t_mat[n, p].astype(BF16), block_rows(xs))
            for i in range(gp):
                h = p * gp + i
                ub_scr[n, h] = sol[:, i * 2 * A_DV:i * 2 * A_DV + A_DV]
                wq_scr[n, h, :c, :] = sol[:, i * 2 * A_DV + A_DV:(i + 1) * 2 * A_DV].astype(wq_scr.dtype)
        yield

    heads = range(A_HEADS)
    for n in range(nc):
        s = [(s_ref if carry else s_in_ref)[st(n, h)] for h in heads]
        s16 = [s[h].astype(BF16) for h in heads]
        wq = [_dot(wq_scr[n, h].astype(BF16), s16[h]) for h in heads]
        yield
        u = [ub_scr[n, h] - wq[h][:c] for h in heads]
        for p in range(npk):
            hs = [p * gp + i for i in range(gp)]
            o_pk = _dot(qk_scr[n, p].astype(BF16), block_rows([u[h] for h in hs]))
            o_ref[0, n * c:(n + 1) * c, p * gp * A_DV:(p + 1) * gp * A_DV] = (
                jnp.concatenate([wq[h][c:] for h in hs], axis=1) + o_pk)
        ku = [_dot_tn(kd_scr[n, h].astype(BF16), u[h].astype(BF16)) for h in heads]
        for h in heads:
            s_ref[st(n, h)] = s[h] * jnp.exp(gc[n][c - 1:c, h:h + 1]) + ku[h]
        yield


def _delta_kernel(qkv_ref, gb_ref, s0_ref, o_ref, s_ref, *scratch, tl, c, carry):
    if carry:
        @pl.when(pl.program_id(1) == 0)
        def _():
            s_ref[...] = s0_ref[...]
    _run(_delta_stages(qkv_ref, gb_ref, s0_ref, o_ref, s_ref, *scratch,
                       tl=tl, c=c, carry=carry, s1_group=tl // c))


def _delta(qkv, gb, s0, tl, c, carry):
    b, l, _ = qkv.shape
    assert carry or l == tl
    tok = lambda n: pl.BlockSpec((1, tl, n), lambda i, j: (i, j, 0))
    state = pl.BlockSpec((1 if carry else tl // c, A_HEADS, A_DK, A_DV), lambda i, j: (i, 0, 0, 0))
    return pl.pallas_call(
        functools.partial(_delta_kernel, tl=tl, c=c, carry=carry),
        grid=(b, l // tl),
        in_specs=[tok(A_CONV_CH), tok(2 * LANES), state],
        out_specs=[tok(A_VD), state],
        out_shape=[jax.ShapeDtypeStruct((b, l, A_VD), F32), jax.ShapeDtypeStruct(s0.shape, F32)],
        scratch_shapes=_delta_scratch(tl, c),
        compiler_params=_params(2),
        name="delta",
    )(qkv, gb, s0)


def _mixer_a_kernel(x_ref, cb_ref, s0_ref, ng_ref, wqkv_ref, wz_ref, wab_ref, cw_ref, gp_ref,
                    o_ref, z_ref, s_ref, c_ref, hist_scr, qkv_a, gb_a, qkv_b, gb_b, *scratch,
                    tl, c, nt, n_tiles):
    k = pl.program_id(0)
    tile_p = jnp.minimum(k, n_tiles - 1)
    tile_d = jnp.maximum(k - 1, 0)

    @pl.when(k == 0)
    def _():
        qkv_b[...] = jnp.zeros_like(qkv_b)
        gb_b[...] = jnp.zeros_like(gb_b)

    @pl.when(tile_p % nt == 0)
    def _():
        _init_history(hist_scr, cb_ref)

    @pl.when(tile_d % nt == 0)
    def _():
        s_ref[...] = s0_ref[...]

    def body(write, read):
        xn = _rms(x_ref[0], ng_ref[...]).astype(BF16)
        proj = _proj_conv_stages(xn, 1, tl, hist_scr, c_ref, wqkv_ref, wz_ref, wab_ref, cw_ref, gp_ref,
                                 write[0], z_ref, write[1])
        delta = _delta_stages(*read, s0_ref, o_ref, s_ref, *scratch, tl=tl, c=c, carry=True,
                              s1_group=MIXER_S1_GROUP)
        _weave(delta, proj, WEAVE_EVERY)

    @pl.when(k % 2 == 0)
    def _():
        body((qkv_a, gb_a), (qkv_b, gb_b))

    @pl.when(k % 2 == 1)
    def _():
        body((qkv_b, gb_b), (qkv_a, gb_a))


def _mixer_a(x, cbuf, s0, w, tl, c):
    b, l, _ = x.shape
    nt = l // tl
    n_tiles = b * nt
    p_idx = lambda k: jnp.minimum(k, n_tiles - 1)
    d_idx = lambda k: jnp.maximum(k - 1, 0)
    consts = _proj_consts(w)
    hand_off = [pltpu.VMEM((1, tl, A_CONV_CH), BF16), pltpu.VMEM((1, tl, 2 * LANES), F32)]
    return pl.pallas_call(
        functools.partial(_mixer_a_kernel, tl=tl, c=c, nt=nt, n_tiles=n_tiles),
        grid=(n_tiles + 1,),
        in_specs=[pl.BlockSpec((1, tl, D_MODEL), lambda k: (p_idx(k) // nt, p_idx(k) % nt, 0)),
                  pl.BlockSpec((1, CONV_W - 1, A_CONV_CH), lambda k: (p_idx(k) // nt, 0, 0)),
                  pl.BlockSpec((1, A_HEADS, A_DK, A_DV), lambda k: (d_idx(k) // nt, 0, 0, 0))]
                 + [_const_spec(cst) for cst in consts],
        out_specs=[pl.BlockSpec((1, tl, A_VD), lambda k: (d_idx(k) // nt, d_idx(k) % nt, 0)),
                   pl.BlockSpec((1, tl, A_VD), lambda k: (p_idx(k) // nt, p_idx(k) % nt, 0)),
                   pl.BlockSpec((1, A_HEADS, A_DK, A_DV), lambda k: (d_idx(k) // nt, 0, 0, 0)),
                   pl.BlockSpec((1, CONV_W - 1, A_CONV_CH), lambda k: (p_idx(k) // nt, 0, 0))],
        out_shape=[jax.ShapeDtypeStruct((b, l, A_VD), F32),
                   jax.ShapeDtypeStruct((b, l, A_VD), BF16),
                   jax.ShapeDtypeStruct((b, A_HEADS, A_DK, A_DV), F32),
                   jax.ShapeDtypeStruct((b, CONV_W - 1, A_CONV_CH), F32)],
        scratch_shapes=[pltpu.VMEM((1, SUBLANES, A_CONV_CH), F32)] + hand_off + hand_off
                       + _delta_scratch(tl, c),
        compiler_params=_params(1),
        name="mixer_a",
    )(x, cbuf, s0, *consts)


def _post_stages(get_mix, res_ref, pe_ref, wmix_ref, fg_ref, wgu_ref, wd_ref,
                 pg_ref, wpg_ref, wpp_ref, n_chunks, finish):
    h = res_ref[...] + _dot(get_mix(), wmix_ref[...])
    xn = _rms(h, fg_ref[...]).astype(BF16)
    yield
    fc = FFN_HIDDEN // n_chunks
    acc = h
    for i in range(n_chunks):
        gt = _dot(xn, wgu_ref[:, i * fc:(i + 1) * fc])
        up = _dot(xn, wgu_ref[:, FFN_HIDDEN + i * fc:FFN_HIDDEN + (i + 1) * fc])
        yield
        acc = acc + _dot((_silu(gt) * up).astype(BF16), wd_ref[i * fc:(i + 1) * fc, :])
        yield
    h2 = acc
    gate = _sigmoid(_dot(_rms(h2, pg_ref[...]).astype(BF16), wpg_ref[...]))
    yield
    finish(h2 + _dot(pe_ref[...].astype(BF16), wpp_ref[...]) * gate)


def _post_mid_kernel(o_nxt_ref, z_nxt_ref, o_first_ref, z_first_ref, res_ref, pe_ref, og_ref,
                     wmix_ref, fg_ref, wgu_ref, wd_ref, pg_ref, wpg_ref, wpp_ref,
                     kvn_ref, kvw_ref, bn_ref, wq_ref, h_ref, q_ref, kv_ref, mix_scr, *, n_chunks):
    def gate_into_scratch(o_ref, z_ref):
        for hd in range(A_HEADS):
            vs = slice(hd * A_DV, (hd + 1) * A_DV)
            oh = o_ref[:, vs]
            oh = oh * lax.rsqrt(jnp.mean(oh * oh, axis=-1, keepdims=True) + EPS) * og_ref[...]
            mix_scr[:, vs] = (oh * _silu(z_ref[:, vs].astype(F32))).astype(BF16)

    @pl.when(pl.program_id(0) == 0)
    def _():
        gate_into_scratch(o_first_ref, z_first_ref)

    def finish(h3):
        h_ref[...] = h3
        kv_ref[...] = _dot(_rms(h3, kvn_ref[...]).astype(BF16), kvw_ref[...])
        q = _dot(_rms(h3, bn_ref[...]).astype(BF16), wq_ref[...]) * (B_HEAD_DIM ** -0.5)
        q_ref[...] = q.astype(q_ref.dtype)

    stages = _post_stages(lambda: mix_scr[...], res_ref, pe_ref, wmix_ref, fg_ref, wgu_ref, wd_ref,
                          pg_ref, wpg_ref, wpp_ref, n_chunks, finish)
    next(stages)
    gate_into_scratch(o_nxt_ref, z_nxt_ref)
    _run(stages)


def _final_norm(fn_ref, y_ref):
    def finish(h3):
        y_ref[...] = _rms(h3, fn_ref[...])
    return finish


def _post_last_kernel(mix_ref, res_ref, pe_ref, wmix_ref, fg_ref, wgu_ref, wd_ref,
                      pg_ref, wpg_ref, wpp_ref, fn_ref, y_ref, *, n_chunks):
    _run(_post_stages(lambda: mix_ref[...], res_ref, pe_ref, wmix_ref, fg_ref, wgu_ref, wd_ref,
                      pg_ref, wpg_ref, wpp_ref, n_chunks, _final_norm(fn_ref, y_ref)))


def _post(mix, res, pe, layer, common, tail, tm, last, q_dtype=BF16):
    m = res.shape[0]
    row = lambda n: pl.BlockSpec((tm, n), lambda i: (i, 0))
    weights = list(common) + list(tail)
    pe_spec = pl.BlockSpec((None, tm, PLE_DIM), lambda i: (layer, i, 0))
    if last:
        rows_in, gain, scratch = [mix], [], []
        mix_specs = [row(mix.shape[1])]
        kern = _post_last_kernel
        out_specs = [row(D_MODEL)]
        out_shape = [jax.ShapeDtypeStruct((m, D_MODEL), F32)]
    else:
        o_raw, z, out_g = mix
        rows_in, gain = [o_raw, z, o_raw, z], [out_g]
        nxt = lambda n: pl.BlockSpec((tm, n), lambda i: (jnp.minimum(i + 1, m // tm - 1), 0))
        first = lambda n: pl.BlockSpec((tm, n), lambda i: (0, 0), pipeline_mode=pl.Buffered(1))
        mix_specs = [nxt(A_VD), nxt(A_VD), first(A_VD), first(A_VD)]
        scratch = [pltpu.VMEM((tm, A_VD), BF16)]
        kern = _post_mid_kernel
        out_specs = [row(D_MODEL), row(B_QD), row(2 * B_KVD)]
        out_shape = [jax.ShapeDtypeStruct((m, D_MODEL), F32),
                     jax.ShapeDtypeStruct((m, B_QD), q_dtype),
                     jax.ShapeDtypeStruct((m, 2 * B_KVD), F32)]
    in_specs = (mix_specs + [row(D_MODEL), pe_spec] + [_const_spec(a) for a in gain]
                + [_const_spec(a, lyr) for a, lyr in weights])
    return pl.pallas_call(
        functools.partial(kern, n_chunks=FFN_CHUNKS),
        grid=(m // tm,),
        in_specs=in_specs,
        out_specs=out_specs,
        out_shape=out_shape,
        scratch_shapes=scratch,
        compiler_params=_params(1),
        name="post_last" if last else "post_mid",
    )(*rows_in, res, pe, *gain, *[a for a, _ in weights])


def _attn_block_stages(sink_ref, q_at, kv_own, kv_prev, blk_idx, put):
    rows = B_GROUP * WINDOW
    k2 = jnp.concatenate([kv_own[:, :B_KVD], kv_prev[:, :B_KVD]], axis=0).astype(BF16)
    v2 = jnp.concatenate([kv_own[:, B_KVD:], kv_prev[:, B_KVD:]], axis=0).astype(BF16)
    qi = lax.broadcasted_iota(jnp.int32, (rows, WINDOW), 0) % WINDOW
    ci = lax.broadcasted_iota(jnp.int32, (rows, WINDOW), 1)
    own = ci <= qi
    prev_ok = (ci > qi) & ((blk_idx - 1) * WINDOW + ci >= 0)
    grp = lax.broadcasted_iota(jnp.int32, (rows, 1), 0) // WINDOW
    kv_heads = range(B_KV_HEADS)
    sc = []
    for j in kv_heads:
        qs = jnp.concatenate([q_at(j * B_GROUP + g) for g in range(B_GROUP)], axis=0)
        sc.append(_dot_nt(qs, k2[:, j * B_HEAD_DIM:(j + 1) * B_HEAD_DIM]))
    yield
    pv = []
    for j in kv_heads:
        sink = jnp.full((rows, 1), sink_ref[j * B_GROUP + B_GROUP - 1], F32)
        for g in range(B_GROUP - 2, -1, -1):
            sink = jnp.where(grp == g, sink_ref[j * B_GROUP + g], sink)
        s = jnp.where(own, sc[j][:, :WINDOW], jnp.where(prev_ok, sc[j][:, WINDOW:], NEG_INF))
        mx = jnp.maximum(jnp.max(s, axis=-1, keepdims=True), sink)
        p = jnp.exp(s - mx)
        den = jnp.sum(p, axis=-1, keepdims=True) + jnp.exp(sink - mx)
        p2 = jnp.concatenate([jnp.where(own, p, 0.0), jnp.where(own, 0.0, p)], axis=1).astype(BF16)
        pv.append((p2, den))
    outs = []
    for j in kv_heads:
        o = _dot(pv[j][0], v2[:, j * B_HEAD_DIM:(j + 1) * B_HEAD_DIM]) / pv[j][1]
        outs += [o[g * WINDOW:(g + 1) * WINDOW, :] for g in range(B_GROUP)]
    put(jnp.concatenate(outs, axis=1))
    yield


def _attn_tile_stages(sink_ref, q_ref, kv_ref, kvp_ref, first_blk, att_ref, tm):
    for j in range(tm // WINDOW):
        r = slice(j * WINDOW, (j + 1) * WINDOW)
        kv_prev = kvp_ref[...] if j == 0 else kv_ref[(j - 1) * WINDOW:j * WINDOW, :]

        def put(o, r=r):
            att_ref[r, :] = o.astype(att_ref.dtype)

        yield from _attn_block_stages(
            sink_ref, lambda h, r=r: q_ref[r, h * B_HEAD_DIM:(h + 1) * B_HEAD_DIM],
            kv_ref[r, :], kv_prev, first_blk + j, put)


def _layer_b_kernel(sink_ref, q_ref, kv_ref, kvp_ref, res_ref, pe_ref, wmix_ref, fg_ref, wgu_ref, wd_ref,
                    pg_ref, wpg_ref, wpp_ref, fn_ref, y_ref, att_a, att_b, *, tm, nt, n_tiles):
    k = pl.program_id(0)
    first_blk = (jnp.minimum(k, n_tiles - 1) % nt) * (tm // WINDOW)

    @pl.when(k == 0)
    def _():
        att_b[...] = jnp.zeros_like(att_b)

    def body(write, read):
        attn = _attn_tile_stages(sink_ref, q_ref, kv_ref, kvp_ref, first_blk, write, tm)
        post = _post_stages(lambda: read[...], res_ref, pe_ref, wmix_ref, fg_ref, wgu_ref, wd_ref,
                            pg_ref, wpg_ref, wpp_ref, FFN_CHUNKS, _final_norm(fn_ref, y_ref))
        _weave(attn, post, 1)

    @pl.when(k % 2 == 0)
    def _():
        body(att_a, att_b)

    @pl.when(k % 2 == 1)
    def _():
        body(att_b, att_a)


def _layer_b(q, kv, sinks, res, pe, layer, common, tail, tm, l):
    m = res.shape[0]
    nt = l // tm
    n_tiles = m // tm
    bpt = tm // WINDOW
    p_idx = lambda k: jnp.minimum(k, n_tiles - 1)
    d_idx = lambda k: jnp.maximum(k - 1, 0)
    weights = list(common) + list(tail)
    return pl.pallas_call(
        functools.partial(_layer_b_kernel, tm=tm, nt=nt, n_tiles=n_tiles),
        grid=(n_tiles + 1,),
        in_specs=[pl.BlockSpec(memory_space=pltpu.SMEM),
                  pl.BlockSpec((tm, B_QD), lambda k: (p_idx(k), 0)),
                  pl.BlockSpec((tm, 2 * B_KVD), lambda k: (p_idx(k), 0)),
                  pl.BlockSpec((WINDOW, 2 * B_KVD), lambda k: (jnp.maximum(p_idx(k) * bpt - 1, 0), 0)),
                  pl.BlockSpec((tm, D_MODEL), lambda k: (d_idx(k), 0)),
                  pl.BlockSpec((None, tm, PLE_DIM), lambda k: (layer, d_idx(k), 0))]
                 + [_const_spec(a, lyr) for a, lyr in weights],
        out_specs=pl.BlockSpec((tm, D_MODEL), lambda k: (d_idx(k), 0)),
        out_shape=jax.ShapeDtypeStruct((m, D_MODEL), F32),
        scratch_shapes=[pltpu.VMEM((tm, B_QD), BF16), pltpu.VMEM((tm, B_QD), BF16)],
        compiler_params=_params(1),
        name="layer_b",
    )(sinks, q, kv, kv, res, pe, *[a for a, _ in weights])


def _attn_sample_kernel(sink_ref, q_ref, kv_ref, ck_ref, cv_ref, o_ref, *, bb, t, nbuf):
    rows = B_GROUP * t
    qi = lax.broadcasted_iota(jnp.int32, (rows, nbuf + t), 0) % t
    ki = lax.broadcasted_iota(jnp.int32, (rows, nbuf + t), 1)
    rel = qi + nbuf - ki
    mask = (rel >= 0) & (rel < WINDOW)
    grp = lax.broadcasted_iota(jnp.int32, (rows, 1), 0) // t
    items = [(b, j) for b in range(bb) for j in range(B_KV_HEADS)]
    sink_col = {}
    for j in range(B_KV_HEADS):
        col = jnp.full((rows, 1), sink_ref[j * B_GROUP + B_GROUP - 1], F32)
        for g in range(B_GROUP - 2, -1, -1):
            col = jnp.where(grp == g, sink_ref[j * B_GROUP + g], col)
        sink_col[j] = col
    keys, vals = {}, {}
    for b in range(bb):
        kv = kv_ref[b]
        keys[b] = jnp.concatenate([ck_ref[b], kv[:, :B_KVD]], axis=0).astype(BF16)
        vals[b] = jnp.concatenate([cv_ref[b], kv[:, B_KVD:]], axis=0).astype(BF16)
    s = {}
    for b, j in items:
        qs = jnp.concatenate(
            [q_ref[b, :, (j * B_GROUP + g) * B_HEAD_DIM:(j * B_GROUP + g + 1) * B_HEAD_DIM]
             for g in range(B_GROUP)], axis=0).astype(BF16)
        s[b, j] = jnp.where(mask, _dot_nt(qs, keys[b][:, j * B_HEAD_DIM:(j + 1) * B_HEAD_DIM]), NEG_INF)
    p, den = {}, {}
    for b, j in items:
        mx = jnp.maximum(jnp.max(s[b, j], axis=-1, keepdims=True), sink_col[j])
        p[b, j] = jnp.exp(s[b, j] - mx)
        den[b, j] = jnp.sum(p[b, j], axis=-1, keepdims=True) + jnp.exp(sink_col[j] - mx)
    o = {}
    for b, j in items:
        o[b, j] = _dot(p[b, j].astype(BF16), vals[b][:, j * B_HEAD_DIM:(j + 1) * B_HEAD_DIM]) / den[b, j]
    for b in range(bb):
        o_ref[b] = jnp.concatenate(
            [o[b, j][g * t:(g + 1) * t, :] for j in range(B_KV_HEADS) for g in range(B_GROUP)],
            axis=1).astype(o_ref.dtype)


def _attn_sample(q, kv, ck, cv, sinks, bb):
    b, t, _ = q.shape
    nbuf = ck.shape[1]
    per_b = lambda r, n: pl.BlockSpec((bb, r, n), lambda i: (i, 0, 0))
    return pl.pallas_call(
        functools.partial(_attn_sample_kernel, bb=bb, t=t, nbuf=nbuf),
        grid=(b // bb,),
        in_specs=[pl.BlockSpec(memory_space=pltpu.SMEM),
                  per_b(t, B_QD), per_b(t, 2 * B_KVD), per_b(nbuf, B_KVD), per_b(nbuf, B_KVD)],
        out_specs=per_b(t, B_QD),
        out_shape=jax.ShapeDtypeStruct((b, t, B_QD), BF16),
        compiler_params=_params(1),
        name="attn_sample",
    )(sinks, q, kv, ck, cv)


def _pad_lanes(v):
    return jnp.pad(v, ((0, 0), (0, LANES - v.shape[1])))


def _prepare(p):
    w_in = p["a_w_in"][0]
    n_gate = A_CONV_CH + A_VD
    row = lambda v: v.reshape(1, -1).astype(F32)
    gu, down = p["ffn_w_gu"].astype(BF16), p["ffn_w_down"].astype(BF16)
    ple_gate, ple_proj = p["ple_w_gate"].astype(BF16), p["ple_w_proj"].astype(BF16)

    def post_common(i, w_mix):
        return [(w_mix.astype(BF16), None), (row(p["ffn_norm"][i]), None), (gu, i), (down, i),
                (row(p["ple_norm"][i]), None), (ple_gate, i), (ple_proj, i)]

    return {
        "a_norm": row(p["a_norm"][0]),
        "w_qkv": w_in[:, :A_CONV_CH].astype(BF16),
        "w_z": w_in[:, A_CONV_CH:n_gate].astype(BF16),
        "w_ab": jnp.concatenate([_pad_lanes(w_in[:, n_gate:n_gate + A_HEADS]),
                                 _pad_lanes(w_in[:, n_gate + A_HEADS:])], axis=1).astype(BF16),
        "conv_w": p["a_conv_w"][0],
        "gate_prm": jnp.concatenate([_pad_lanes(row(p["a_a_log"][0])),
                                     _pad_lanes(row(p["a_dt_bias"][0]))], axis=0),
        "out_g": row(p["a_out_norm"][0]),
        "post0": post_common(0, p["a_w_out"][0]),
        "tail0": [(row(p["kv_norm"]), None), (p["kv_w"].astype(BF16), None),
                  (row(p["b_norm"][0]), None), (p["b_w_q"][0].astype(BF16), None)],
        "post1": post_common(1, p["b_w_o"][0]),
        "tail1": [(row(p["final_norm"]), None)],
        "sinks": p["b_sinks"][0].astype(F32),
    }


def _tiles(b, l, prompt):
    m = b * l
    c = min(CHUNK, l)
    if prompt:
        return dict(tm=min(512, m), mx_tl=min(512, l), c=c)
    return dict(tm=min(256, m), pc_bb=min(32, b), dl_tl=min(8 * l, m), c=c, at_bb=min(8, b))


def _trunk(x, pe, s_init, c_init, k_buf, v_buf, w):
    b, l, _ = x.shape
    m = b * l
    prompt = k_buf is None
    ts = _tiles(b, l, prompt)
    act_dtype = BF16 if prompt else F32
    if prompt:
        o, z, s_new, c_new = _mixer_a(x, c_init, s_init, w, ts["mx_tl"], ts["c"])
    else:
        qkv, z, gb, c_new = _proj_conv(x, c_init, w, ts["pc_bb"], l, act_dtype)
        grp = lambda a: a.reshape(m // ts["dl_tl"], ts["dl_tl"], a.shape[-1])
        o, s_new = _delta(grp(qkv), grp(gb), s_init, ts["dl_tl"], ts["c"], False)
    x2d = x.reshape(m, D_MODEL)
    pe2d = pe.reshape(pe.shape[0], m, PLE_DIM)
    h, q, kv = _post((o.reshape(m, -1), z.reshape(m, -1), w["out_g"]), x2d, pe2d, 0,
                     w["post0"], w["tail0"], ts["tm"], False, act_dtype)
    kv3 = kv.reshape(b, l, -1)
    heads = lambda a: a.reshape(a.shape[0], a.shape[1], B_KV_HEADS, B_HEAD_DIM)
    if prompt:
        y = _layer_b(q, kv, w["sinks"], h, pe2d, 1, w["post1"], w["tail1"], ts["tm"], l)
        last = kv3[:, l - WINDOW:]
        k_win, v_win = heads(last[:, :, :B_KVD]), heads(last[:, :, B_KVD:])
    else:
        nbuf = k_buf.shape[1]
        att = _attn_sample(q.reshape(b, l, -1), kv3, k_buf.reshape(b, nbuf, B_KVD),
                           v_buf.reshape(b, nbuf, B_KVD), w["sinks"], ts["at_bb"])
        k_win = jnp.concatenate([k_buf, heads(kv3[:, :, :B_KVD])], axis=1)[:, -nbuf:]
        v_win = jnp.concatenate([v_buf, heads(kv3[:, :, B_KVD:])], axis=1)[:, -nbuf:]
        (y,) = _post(att.reshape(m, -1), h, pe2d, 1, w["post1"], w["tail1"], ts["tm"], True)
    return y.reshape(b, l, D_MODEL), s_new[None], c_new[None], k_win, v_win


def kernel(x_prompt, x_sample, state_delta, state_conv, cache_k_win, cache_v_win, p_prompt, p_sample, a_norm, a_w_in, a_conv_w, a_a_log, a_dt_bias, a_out_norm, a_w_out, kv_norm, kv_w, b_norm, b_w_q, b_sinks, b_w_o, ffn_norm, ffn_w_gu, ffn_w_down, ple_norm, ple_w_proj, ple_w_gate, final_norm):
    w = _prepare({
        "a_norm": a_norm, "a_w_in": a_w_in, "a_conv_w": a_conv_w, "a_a_log": a_a_log,
        "a_dt_bias": a_dt_bias, "a_out_norm": a_out_norm, "a_w_out": a_w_out,
        "kv_norm": kv_norm, "kv_w": kv_w, "b_norm": b_norm, "b_w_q": b_w_q, "b_sinks": b_sinks,
        "b_w_o": b_w_o, "ffn_norm": ffn_norm, "ffn_w_gu": ffn_w_gu, "ffn_w_down": ffn_w_down,
        "ple_norm": ple_norm, "ple_w_proj": ple_w_proj, "ple_w_gate": ple_w_gate,
        "final_norm": final_norm,
    })
    bp = x_prompt.shape[0]
    s0 = jnp.zeros((bp, A_HEADS, A_DK, A_DV), F32)
    c0 = jnp.zeros((bp, CONV_W - 1, A_CONV_CH), x_prompt.dtype)
    y_p, sd_p, sc_p, kw_p, vw_p = _trunk(x_prompt, p_prompt, s0, c0, None, None, w)
    y_s, sd_s, sc_s, kw_s, vw_s = _trunk(x_sample, p_sample, state_delta[0], state_conv[0],
                                         cache_k_win, cache_v_win, w)
    return (y_p, y_s, sd_p, sd_s, sc_p, sc_s, kw_p, kw_s, vw_p, vw_s)
```
